```python
import jax, jax.numpy as jnp
from jax import lax
import numpy as np

D_MODEL = 2048
BATCH = 2
SEQ = 4096
DEPTH = 4

HEAD_DIM = 64
N_HEADS = D_MODEL // HEAD_DIM
N_HEADS_A = N_HEADS // 2
N_HEADS_B = N_HEADS - N_HEADS_A
WIDTH_A = N_HEADS_A * HEAD_DIM
WIDTH_B = N_HEADS_B * HEAD_DIM
MIX_WIDTH = N_HEADS * HEAD_DIM
DILATED_PATTERNS = ((128, 1), (512, 4), (2048, 16))
BAND = 128
MOBA_BLOCK = 256
MOBA_TOPK = 3
MOBA_Q_CHUNK = 32
N_EXPERTS = 32
TOP_K = 4
D_FF = 768
SWIGLU_LIMIT = 7.0
SWIGLU_ALPHA = 1.702
EXPERT_BLOCK = 128
ROPE_THETA = 10000.0
NORM_EPS = 1e-6
NEG_INF = -1e30
N_MOD = 6

kernel_name = "hybrid_dilated_moba_moe_adaln"


def rms_norm(x):
    xf = x.astype(jnp.float32)
    return (xf * lax.rsqrt(jnp.mean(xf * xf, axis=-1, keepdims=True) + NORM_EPS)).astype(x.dtype)


def rotary(x, positions):
    inv_freq = 1.0 / (ROPE_THETA ** (jnp.arange(0, HEAD_DIM, 2, dtype=jnp.float32) / HEAD_DIM))
    ang = positions[:, None, :, None].astype(jnp.float32) * inv_freq
    cos, sin = jnp.cos(ang), jnp.sin(ang)
    xf = x.astype(jnp.float32)
    x1, x2 = xf[..., :HEAD_DIM // 2], xf[..., HEAD_DIM // 2:]
    return jnp.concatenate([x1 * cos - x2 * sin, x2 * cos + x1 * sin], axis=-1).astype(x.dtype)


def dilated_pattern(q, k, v, window, dilation):
    B, H, S, dh = q.shape
    steps = window // dilation
    span = dilation * BAND
    Sp = -(-S // span) * span
    L = Sp // dilation
    nb = L // BAND
    pad = ((0, 0), (0, 0), (0, Sp - S), (0, 0))

    def to_blocks(t):
        t = jnp.pad(t, pad).reshape(B, H, L, dilation, dh).transpose(0, 1, 3, 2, 4)
        return t.reshape(B, H, dilation, nb, BAND, dh)

    def with_prev(t):
        prev = jnp.concatenate([jnp.zeros_like(t[:, :, :, :1]), t[:, :, :, :-1]], axis=3)
        return jnp.concatenate([prev, t], axis=4)

    qb = to_blocks(q)
    kk = with_prev(to_blocks(k))
    vv = with_prev(to_blocks(v))
    s = jnp.einsum('bhrnqd,bhrnkd->bhrnqk', qb, kk).astype(jnp.float32)
    qi = jnp.arange(BAND)[:, None]
    kj = jnp.arange(2 * BAND)[None, :]
    dist = qi + BAND - kj
    blk = jnp.arange(nb)[:, None, None]
    valid = (dist >= 0) & (dist <= steps) & (blk * BAND + kj - BAND >= 0)
    s = jnp.where(valid, s, NEG_INF)
    m = jnp.max(s, axis=-1, keepdims=True)
    p = jnp.exp(s - m)
    den = jnp.sum(p, axis=-1, keepdims=True)
    o = jnp.einsum('bhrnqk,bhrnkd->bhrnqd', (p / den).astype(v.dtype), vv)
    lse = (m + jnp.log(den))[..., 0]
    o = o.reshape(B, H, dilation, L, dh).transpose(0, 1, 3, 2, 4).reshape(B, H, Sp, dh)[:, :, :S]
    lse = lse.reshape(B, H, dilation, L).transpose(0, 1, 3, 2).reshape(B, H, Sp)[:, :, :S]
    return o, lse


def dilated_attention(q, k, v):
    outs, lses = [], []
    for window, dilation in DILATED_PATTERNS:
        o, lse = dilated_pattern(q, k, v, window, dilation)
        outs.append(o)
        lses.append(lse)
    wts = jax.nn.softmax(jnp.stack(lses), axis=0)
    return jnp.einsum('pbhs,pbhsd->bhsd', wts.astype(v.dtype), jnp.stack(outs))


def moba_attention(q, k, v):
    B, H, S, dh = q.shape
    Sp = -(-S // MOBA_BLOCK) * MOBA_BLOCK
    pad = ((0, 0), (0, 0), (0, Sp - S), (0, 0))
    qp, kp, vp = jnp.pad(q, pad), jnp.pad(k, pad), jnp.pad(v, pad)
    nblk = Sp // MOBA_BLOCK
    ksel = min(MOBA_TOPK, nblk)
    kblk = kp.reshape(B, H, nblk, MOBA_BLOCK, dh)
    vblk = vp.reshape(B, H, nblk, MOBA_BLOCK, dh)
    kmean = jnp.mean(kblk.astype(jnp.float32), axis=3)
    nc = Sp // MOBA_Q_CHUNK
    q_chunks = qp.reshape(B, H, nc, MOBA_Q_CHUNK, dh).transpose(2, 0, 1, 3, 4)
    bi = jnp.arange(B)[:, None, None, None]
    hi = jnp.arange(H)[None, :, None, None]

    def chunk(args):
        i, qc = args
        t = i * MOBA_Q_CHUNK + jnp.arange(MOBA_Q_CHUNK)
        own = (i * MOBA_Q_CHUNK) // MOBA_BLOCK
        gate = jnp.einsum('bhqd,bhnd->bhqn', qc.astype(jnp.float32), kmean)
        gate = jnp.where(jnp.arange(nblk) < own, gate, NEG_INF)
        _, sel = lax.top_k(gate, ksel)
        sel_valid = sel < own
        k_sel = kblk[bi, hi, sel]
        v_sel = vblk[bi, hi, sel]
        s_sel = jnp.einsum('bhqd,bhqnkd->bhqnk', qc, k_sel).astype(jnp.float32)
        s_sel = jnp.where(sel_valid[..., None], s_sel, NEG_INF).reshape(B, H, MOBA_Q_CHUNK, ksel * MOBA_BLOCK)
        k_own = lax.dynamic_index_in_dim(kblk, own, axis=2, keepdims=False)
        v_own = lax.dynamic_index_in_dim(vblk, own, axis=2, keepdims=False)
        s_own = jnp.einsum('bhqd,bhkd->bhqk', qc, k_own).astype(jnp.float32)
        causal = own * MOBA_BLOCK + jnp.arange(MOBA_BLOCK)[None, :] <= t[:, None]
        s_own = jnp.where(causal, s_own, NEG_INF)
        p = jax.nn.softmax(jnp.concatenate([s_sel, s_own], axis=-1), axis=-1).astype(v.dtype)
        p_sel = p[..., :ksel * MOBA_BLOCK].reshape(B, H, MOBA_Q_CHUNK, ksel, MOBA_BLOCK)
        p_own = p[..., ksel * MOBA_BLOCK:]
        return (jnp.einsum('bhqnk,bhqnkd->bhqd', p_sel, v_sel)
                + jnp.einsum('bhqk,bhkd->bhqd', p_own, v_own))

    out = lax.map(chunk, (jnp.arange(nc), q_chunks))
    return out.transpose(1, 2, 0, 3, 4).reshape(B, H, Sp, dh)[:, :, :S]


def routed_ffn(h, w_router, b_router, w_up, b_up, w_down, b_down):
    B, S, D = h.shape
    T = B * S
    xt = h.reshape(T, D)
    logits = (xt @ w_router + b_router).astype(jnp.float32)
    top_v, top_e = lax.top_k(logits, TOP_K)
    gates = jax.nn.softmax(top_v, axis=-1)
    A = T * TOP_K
    e_flat = top_e.reshape(A)
    tok_flat = jnp.broadcast_to(jnp.arange(T, dtype=jnp.int32)[:, None], (T, TOP_K)).reshape(A)
    g_flat = gates.reshape(A)
    order = jnp.argsort(e_flat)
    e_s, tok_s, g_s = e_flat[order], tok_flat[order], g_flat[order]
    counts = jnp.bincount(e_flat, length=N_EXPERTS)
    padded = ((counts + EXPERT_BLOCK - 1) // EXPERT_BLOCK) * EXPERT_BLOCK
    pend = jnp.cumsum(padded)
    pstart = pend - padded
    start = jnp.cumsum(counts) - counts
    dest = pstart[e_s] + (jnp.arange(A) - start[e_s])
    n_slots = A + N_EXPERTS * EXPERT_BLOCK
    n_blocks = n_slots // EXPERT_BLOCK
    buf_tok = jnp.zeros((n_slots,), jnp.int32).at[dest].set(tok_s)
    buf_g = jnp.zeros((n_slots,), jnp.float32).at[dest].set(g_s)
    block_e = jnp.clip(jnp.searchsorted(pend, jnp.arange(n_blocks) * EXPERT_BLOCK, side='right'), 0, N_EXPERTS - 1)

    def expert_block(args):
        e, toks = args
        u = xt[toks] @ w_up[e] + b_up[e]
        glu = jnp.minimum(u[:, :D_FF], SWIGLU_LIMIT)
        lin = jnp.clip(u[:, D_FF:], -SWIGLU_LIMIT, SWIGLU_LIMIT)
        act = glu * jax.nn.sigmoid(SWIGLU_ALPHA * glu) * (lin + 1)
        return act @ w_down[e] + b_down[e]

    outs = lax.map(expert_block, (block_e, buf_tok.reshape(n_blocks, EXPERT_BLOCK)))
    y = jnp.zeros((T, D), h.dtype).at[buf_tok].add(outs.reshape(n_slots, D) * buf_g[:, None].astype(h.dtype))
    return y.reshape(B, S, D)


def hybrid_layer(x, silu_c, positions, w_ada, b_ada, w_in, beta_a, beta_b, w_out,
                 w_router, b_router, w_up, b_up, w_down, b_down):
    B, S, _ = x.shape
    mod = (silu_c @ w_ada + b_ada).reshape(B, N_MOD, D_MODEL)[:, :, None, :]
    shift_a, scale_a, gate_a, shift_f, scale_f, gate_f = [mod[:, i] for i in range(N_MOD)]
    h = rms_norm(x) * (1 + scale_a) + shift_a
    qkv = (h @ w_in).reshape(B, S, 3, N_HEADS, HEAD_DIM).transpose(2, 0, 3, 1, 4)
    q = rotary(qkv[0], positions) * (HEAD_DIM ** -0.5)
    k = rotary(qkv[1], positions)
    v = qkv[2]
    o_a = dilated_attention(q[:, :N_HEADS_A], k[:, :N_HEADS_A], v[:, :N_HEADS_A])
    o_b = moba_attention(q[:, N_HEADS_A:], k[:, N_HEADS_A:], v[:, N_HEADS_A:])
    y_a = rms_norm(o_a.transpose(0, 2, 1, 3).reshape(B, S, WIDTH_A)) * beta_a
    y_b = rms_norm(o_b.transpose(0, 2, 1, 3).reshape(B, S, WIDTH_B)) * beta_b
    mixed = jnp.concatenate([y_a, y_b], axis=-1) @ w_out
    x = x + gate_a * mixed
    h = rms_norm(x) * (1 + scale_f) + shift_f
    x = x + gate_f * routed_ffn(h, w_router, b_router, w_up, b_up, w_down, b_down)
    return x


def setup_inputs(seed: int = 0) -> dict:
    key = jax.random.key(seed)
    ks = jax.random.split(key, 16)

    def nrm(k, shape, scale):
        return jax.random.normal(k, shape, jnp.float32) * scale

    x = nrm(ks[0], (BATCH, SEQ, D_MODEL), 1.0)
    c = nrm(ks[1], (BATCH, D_MODEL), 1.0)
    positions = (jnp.arange(SEQ, dtype=jnp.int32)[None, :]
                 + jax.random.randint(ks[2], (BATCH, 1), 0, 1024, dtype=jnp.int32))
    w_ada = nrm(ks[3], (DEPTH, D_MODEL, N_MOD * D_MODEL), 0.5 * D_MODEL ** -0.5)
    b_ada = nrm(ks[4], (DEPTH, N_MOD * D_MODEL), 0.02)
    w_in = nrm(ks[5], (DEPTH, D_MODEL, 3 * MIX_WIDTH), D_MODEL ** -0.5)
    beta_a = 1.0 + nrm(ks[6], (DEPTH, WIDTH_A), 0.05)
    beta_b = 1.0 + nrm(ks[7], (DEPTH, WIDTH_B), 0.05)
    w_out = nrm(ks[8], (DEPTH, MIX_WIDTH, D_MODEL), MIX_WIDTH ** -0.5)
    w_router = nrm(ks[9], (DEPTH, D_MODEL, N_EXPERTS), D_MODEL ** -0.5)
    b_router = nrm(ks[10], (DEPTH, N_EXPERTS), 0.01)
    w_up = nrm(ks[11], (DEPTH, N_EXPERTS, D_MODEL, 2 * D_FF), D_MODEL ** -0.5)
    b_up = nrm(ks[12], (DEPTH, N_EXPERTS, 2 * D_FF), 0.02)
    w_down = nrm(ks[13], (DEPTH, N_EXPERTS, D_FF, D_MODEL), D_FF ** -0.5)
    b_down = nrm(ks[14], (DEPTH, N_EXPERTS, D_MODEL), 0.02)
    g_final = 1.0 + nrm(ks[15], (D_MODEL,), 0.05)
    return {"x": x, "c": c, "positions": positions, "w_ada": w_ada, "b_ada": b_ada,
            "w_in": w_in, "beta_a": beta_a, "beta_b": beta_b, "w_out": w_out,
            "w_router": w_router, "b_router": b_router, "w_up": w_up, "b_up": b_up,
            "w_down": w_down, "b_down": b_down, "g_final": g_final}


def reference(x, c, positions, w_ada, b_ada, w_in, beta_a, beta_b, w_out,
              w_router, b_router, w_up, b_up, w_down, b_down, g_final):
    silu_c = jax.nn.silu(c)
    for l in range(DEPTH):
        x = hybrid_layer(x, silu_c, positions, w_ada[l], b_ada[l], w_in[l], beta_a[l], beta_b[l],
                         w_out[l], w_router[l], b_router[l], w_up[l], b_up[l], w_down[l], b_down[l])
    return rms_norm(x) * g_final
```

```python
import math

import numpy as np
import jax
import jax.numpy as jnp
from jax import lax
from jax.experimental import pallas as pl
from jax.experimental.pallas import tpu as pltpu

HEAD_DIM = 64
LANES = 128
BLK = 256
BAND = 128
DILATIONS = (1, 4, 16)
MOBA_TOPK = 3
TOP_K = 4
SWIGLU_LIMIT = 7.0
SWIGLU_ALPHA = 1.702
ROPE_THETA = 10000.0
NORM_EPS = 1e-6
NEG = -1e30
EXPERT_TILE = 256
VMEM_LIMIT = 60000 * 1024

_ARB = pltpu.ARBITRARY


def _cparams(n_axes):
    return pltpu.CompilerParams(dimension_semantics=(_ARB,) * n_axes, vmem_limit_bytes=VMEM_LIMIT)


def _to_perm(x):
    B, S = x.shape[:2]
    rest = x.shape[2:]
    x = x.reshape(B, S // BLK, 16, 4, 4, *rest)
    x = jnp.swapaxes(x, 2, 4)
    return x.reshape(B, S, *rest)


def _from_perm(x):
    B, S = x.shape[:2]
    rest = x.shape[2:]
    x = x.reshape(B, S // BLK, 4, 4, 16, *rest)
    x = jnp.swapaxes(x, 2, 4)
    return x.reshape(B, S, *rest)


def _local_pos():
    p = np.arange(BLK)
    return 16 * (p % 16) + 4 * ((p // 16) % 4) + p // 64


def _band_bias(seq_q, seq_k):
    dist = seq_q[:, None] - seq_k[None, :]
    ok = (dist >= 0) & (dist <= BAND)
    half = seq_k.shape[0] // 2
    exists = np.arange(seq_k.shape[0])[None, :] >= half
    normal = np.where(ok, 0.0, NEG).astype(np.float32)
    first = np.where(ok & exists, 0.0, NEG).astype(np.float32)
    two = np.stack([first, normal])
    return np.concatenate([two, two], axis=1)


def _dilated_biases():
    lp = _local_pos()
    b1 = _band_bias(lp + BLK, np.concatenate([lp, lp + BLK]))
    hi, c, lo = np.meshgrid(np.arange(4), np.arange(4), np.arange(16), indexing="ij")
    seq_k4 = (64 * hi + 4 * lo + c).reshape(-1)
    seq_q4 = seq_k4[:128] + 128
    b4 = _band_bias(seq_q4, seq_k4)
    b16 = _band_bias(np.arange(128) + 128, np.arange(256))
    return b1, b4, b16


def _own_bias():
    lp = _local_pos()
    ok = lp[None, :] <= lp[:, None]
    m = np.where(ok, 0.0, NEG).astype(np.float32)
    return np.concatenate([m, m], axis=0)


def _stack_heads(q):
    lane = lax.broadcasted_iota(jnp.int32, q.shape, 1)
    zero = jnp.zeros_like(q)
    return jnp.concatenate([jnp.where(lane < HEAD_DIM, q, zero), jnp.where(lane >= HEAD_DIM, q, zero)], axis=0)


def _scores(qs, kk):
    return lax.dot_general(qs, kk, (((1,), (1,)), ((), ())), preferred_element_type=jnp.float32)


def _tile_lanes(x, k):
    return x if k == 1 else jnp.concatenate([x] * k, axis=1)


def _softmax_step(s, vv, m_old, l_old, acc_old):
    m_cur = jnp.max(s, axis=1, keepdims=True)
    m_new = jnp.maximum(m_old, m_cur)
    alpha = jnp.exp(m_old - m_new)
    p = jnp.exp(s - _tile_lanes(m_new, s.shape[1] // LANES))
    l_new = alpha * l_old + jnp.sum(p, axis=1, keepdims=True)
    pv = jnp.dot(p.astype(vv.dtype), vv, preferred_element_type=jnp.float32)
    return m_new, l_new, alpha * acc_old + pv


def _merge_heads(x):
    n = x.shape[0] // 2
    lane = lax.broadcasted_iota(jnp.int32, (n, LANES), 1)
    return jnp.where(lane < HEAD_DIM, x[:n], x[n:])


def _ada_kernel(c_ref, w_ref, b_ref, o_ref):
    c = c_ref[...]
    sc = (c * (1.0 / (1.0 + jnp.exp(-c)))).astype(jnp.bfloat16)
    o_ref[...] = jnp.dot(sc, w_ref[...].astype(jnp.bfloat16), preferred_element_type=jnp.float32) + b_ref[...]


def _ada_mod(c, w_ada, b_ada):
    depth, d, n = w_ada.shape
    rows = 8
    tn = math.gcd(n, 1024)
    cp = jnp.zeros((rows, d), jnp.float32).at[: c.shape[0]].set(c)
    return pl.pallas_call(
        _ada_kernel,
        grid=(depth, n // tn),
        in_specs=[
            pl.BlockSpec((rows, d), lambda l, j: (0, 0)),
            pl.BlockSpec((None, d, tn), lambda l, j: (l, 0, j)),
            pl.BlockSpec((None, 1, tn), lambda l, j: (l, 0, j)),
        ],
        out_specs=pl.BlockSpec((None, rows, tn), lambda l, j: (l, 0, j)),
        out_shape=jax.ShapeDtypeStruct((depth, rows, n), jnp.float32),
        compiler_params=_cparams(2),
        name="ada_mod",
    )(cp, w_ada, b_ada.reshape(depth, 1, n))


def _rms_mod(x, scale, shift):
    ms = jnp.mean(x * x, axis=-1, keepdims=True)
    return x * lax.rsqrt(ms + NORM_EPS) * (1.0 + scale) + shift


def _cast_rows(src_ref, dst_ref):
    rows_per_step = math.gcd(src_ref.shape[0], 256)
    n = src_ref.shape[0] // rows_per_step

    def body(i, carry):
        r = pl.multiple_of(i * rows_per_step, rows_per_step)
        dst_ref[pl.ds(r, rows_per_step), :] = src_ref[pl.ds(r, rows_per_step), :].astype(dst_ref.dtype)
        return carry

    lax.fori_loop(0, n, body, 0)


def _qkv_kernel(x_ref, shift_ref, scale_ref, w_ref, cos_ref, sin_ref, o_ref, wbf_ref):
    b = pl.program_id(1)
    i = pl.program_id(2)

    @pl.when((b == 0) & (i == 0))
    def _():
        _cast_rows(w_ref, wbf_ref)

    h = _rms_mod(x_ref[...], scale_ref[...], shift_ref[...]).astype(jnp.bfloat16)
    acc = jnp.dot(h, wbf_ref[...], preferred_element_type=jnp.float32)
    cos = cos_ref[...]
    sin = sin_ref[...]
    lane = lax.broadcasted_iota(jnp.int32, cos.shape, 1)
    first_half = (lane % HEAD_DIM) < (HEAD_DIM // 2)
    for c in range(o_ref.shape[0]):
        a = acc[:, c * LANES:(c + 1) * LANES]
        partner = jnp.where(first_half, pltpu.roll(a, LANES - HEAD_DIM // 2, 1), pltpu.roll(a, HEAD_DIM // 2, 1))
        o_ref[c] = (a * cos + partner * sin).astype(o_ref.dtype)


def _qkv_proj(x, mod6, w_in, cos_t, sin_t, tm=256):
    B, S, D = x.shape
    n3 = w_in.shape[1]
    width = n3 // 3
    pairs = width // LANES
    return pl.pallas_call(
        _qkv_kernel,
        grid=(3, B, S // tm),
        in_specs=[
            pl.BlockSpec((None, tm, D), lambda j, b, i: (b, i, 0)),
            pl.BlockSpec((None, None, 1, D), lambda j, b, i: (b, 0, 0, 0)),
            pl.BlockSpec((None, None, 1, D), lambda j, b, i: (b, 1, 0, 0)),
            pl.BlockSpec((D, width), lambda j, b, i: (0, j), pipeline_mode=pl.Buffered(1)),
            pl.BlockSpec((None, None, tm, LANES), lambda j, b, i: (j, b, i, 0)),
            pl.BlockSpec((None, None, tm, LANES), lambda j, b, i: (j, b, i, 0)),
        ],
        out_specs=pl.BlockSpec((None, pairs, tm, LANES), lambda j, b, i: (b, j, i, 0)),
        out_shape=jax.ShapeDtypeStruct((B, 3 * pairs, S, LANES), jnp.bfloat16),
        scratch_shapes=[pltpu.VMEM((D, width), jnp.bfloat16)],
        compiler_params=_cparams(3),
        name="qkv_proj",
    )(x, mod6, mod6, w_in, cos_t, sin_t)


def _dilated_kernel(q_ref, k_ref, v_ref, b1_ref, b4_ref, b16_ref, o_ref, m_ref, l_ref, acc_ref):
    nb = q_ref.shape[0]

    def rows(x):
        return x.reshape(-1, LANES)

    def update(q, kk, vv, bias, idx):
        qs = _stack_heads(q)
        s = _scores(qs, kk) + bias
        a_raw = acc_ref[idx]
        shp = a_raw.shape
        m_old = jnp.concatenate([rows(m_ref[(0,) + idx]), rows(m_ref[(1,) + idx])], axis=0)
        l_old = jnp.concatenate([rows(l_ref[(0,) + idx]), rows(l_ref[(1,) + idx])], axis=0)
        a_old = jnp.concatenate([rows(a_raw), rows(a_raw)], axis=0)
        m_new, l_new, a_new = _softmax_step(s, vv, m_old, l_old, a_old)
        n = q.shape[0]
        m_ref[(0,) + idx] = m_new[:n].reshape(shp)
        m_ref[(1,) + idx] = m_new[n:].reshape(shp)
        l_ref[(0,) + idx] = l_new[:n].reshape(shp)
        l_ref[(1,) + idx] = l_new[n:].reshape(shp)
        acc_ref[idx] = _merge_heads(a_new).reshape(shp)

    def init(n, carry):
        m_ref[0, n] = jnp.full(m_ref.shape[2:], NEG, jnp.float32)
        m_ref[1, n] = jnp.full(m_ref.shape[2:], NEG, jnp.float32)
        l_ref[0, n] = jnp.zeros(l_ref.shape[2:], jnp.float32)
        l_ref[1, n] = jnp.zeros(l_ref.shape[2:], jnp.float32)
        acc_ref[n] = jnp.zeros(acc_ref.shape[1:], jnp.float32)
        return carry

    lax.fori_loop(0, nb, init, 0)

    def d1(n, carry):
        prev = jnp.maximum(n - 1, 0)
        kk = jnp.concatenate([rows(k_ref[prev]), rows(k_ref[n])], axis=0)
        vv = jnp.concatenate([rows(v_ref[prev]), rows(v_ref[n])], axis=0)
        update(rows(q_ref[n]), kk, vv, b1_ref[jnp.minimum(n, 1)], (n,))
        return carry

    lax.fori_loop(0, nb, d1, 0)

    per4 = nb // 2

    def d4(t, carry):
        r4 = t // per4
        a = t % per4
        cur = pl.ds(2 * a, 2)
        prev = pl.ds(jnp.maximum(2 * a - 2, 0), 2)
        grp = pl.ds(4 * r4, 4)
        kk = jnp.concatenate([rows(k_ref[prev, grp]), rows(k_ref[cur, grp])], axis=0)
        vv = jnp.concatenate([rows(v_ref[prev, grp]), rows(v_ref[cur, grp])], axis=0)
        update(rows(q_ref[cur, grp]), kk, vv, b4_ref[jnp.minimum(a, 1)], (cur, grp))
        return carry

    lax.fori_loop(0, 4 * per4, d4, 0)

    per16 = nb // 8

    def d16(t, carry):
        rr = t // per16
        b = t % per16
        cur = pl.ds(8 * b, 8)
        prev = pl.ds(jnp.maximum(8 * b - 8, 0), 8)
        kk = jnp.concatenate([rows(k_ref[prev, rr]), rows(k_ref[cur, rr])], axis=0)
        vv = jnp.concatenate([rows(v_ref[prev, rr]), rows(v_ref[cur, rr])], axis=0)
        update(rows(q_ref[cur, rr]), kk, vv, b16_ref[jnp.minimum(b, 1)], (cur, rr))
        return carry

    lax.fori_loop(0, 16 * per16, d16, 0)

    def fin(n, carry):
        lane = lax.broadcasted_iota(jnp.int32, acc_ref.shape[1:], 2)
        den = jnp.where(lane < HEAD_DIM, l_ref[0, n], l_ref[1, n])
        o_ref[n] = (acc_ref[n] / den).astype(o_ref.dtype)
        return carry

    lax.fori_loop(0, nb, fin, 0)


def _seq_view(a):
    B, P, S, L = a.shape
    return a.reshape(B, P, S // BLK, 16, 16, L)


def _dilated_attention(qkv, biases):
    B, P3, S, _ = qkv.shape
    P = P3 // 3
    pa = P // 2
    nb = S // BLK
    assert S % (16 * BAND) == 0, "sequence must be a whole number of dilation-16 band blocks"
    v6 = _seq_view(qkv)
    blk = (None, None, nb, 16, 16, LANES)
    b1, b4, b16 = biases
    out = pl.pallas_call(
        _dilated_kernel,
        grid=(B, pa),
        in_specs=[
            pl.BlockSpec(blk, lambda b, p: (b, p, 0, 0, 0, 0)),
            pl.BlockSpec(blk, lambda b, p: (b, P + p, 0, 0, 0, 0)),
            pl.BlockSpec(blk, lambda b, p: (b, 2 * P + p, 0, 0, 0, 0)),
            pl.BlockSpec(b1.shape, lambda b, p: (0, 0, 0)),
            pl.BlockSpec(b4.shape, lambda b, p: (0, 0, 0)),
            pl.BlockSpec(b16.shape, lambda b, p: (0, 0, 0)),
        ],
        out_specs=pl.BlockSpec(blk, lambda b, p: (b, p, 0, 0, 0, 0)),
        out_shape=jax.ShapeDtypeStruct((B, pa, nb, 16, 16, LANES), jnp.bfloat16),
        scratch_shapes=[
            pltpu.VMEM((2, nb, 16, 16, LANES), jnp.float32),
            pltpu.VMEM((2, nb, 16, 16, LANES), jnp.float32),
            pltpu.VMEM((nb, 16, 16, LANES), jnp.float32),
        ],
        compiler_params=_cparams(2),
        name="dilated_attn",
    )(v6, v6, v6, b1, b4, b16)
    return out.reshape(B, pa, S, LANES)


def _moba_kernel(q_ref, k_ref, v_ref, own_ref, o_ref, kmean_ref, pen_ref, m_ref, l_ref, acc_ref):
    n = pl.program_id(2)
    nb = k_ref.shape[0]

    def rows(x):
        return x.reshape(-1, LANES)

    @pl.when(n == 0)
    def _():
        kmean_ref[...] = jnp.zeros(kmean_ref.shape, jnp.float32)
        for j in range(nb):
            kmean_ref[pl.ds(j, 1), :] = jnp.mean(rows(k_ref[j]).astype(jnp.float32), axis=0, keepdims=True)

    qs = _stack_heads(rows(q_ref[...]))

    gate = _scores(qs, kmean_ref[...].astype(jnp.bfloat16))
    jidx = lax.broadcasted_iota(jnp.int32, gate.shape, 1)
    g = jnp.where(jidx < n, gate, -jnp.inf)
    rank = jnp.zeros(gate.shape, jnp.int32)
    for jp in range(nb):
        col = g[:, jp:jp + 1]
        beats = (col > g) | ((col == g) & (jp < jidx))
        rank = rank + beats.astype(jnp.int32)
    sel = (rank < MOBA_TOPK) & (jidx < n)
    pen = jnp.where(sel, 0.0, NEG).astype(jnp.float32)
    for j in range(nb):
        pen_ref[j] = jnp.broadcast_to(pen[:, j:j + 1], pen_ref.shape[1:])

    s = _scores(qs, rows(k_ref[n])) + own_ref[...]
    m0 = jnp.max(s, axis=1, keepdims=True)
    p = jnp.exp(s - m0)
    m_ref[...] = jnp.broadcast_to(m0, m_ref.shape)
    l_ref[...] = jnp.broadcast_to(jnp.sum(p, axis=1, keepdims=True), l_ref.shape)
    acc_ref[...] = jnp.dot(p.astype(jnp.bfloat16), rows(v_ref[n]), preferred_element_type=jnp.float32)

    def past(j, carry):
        sj = _scores(qs, rows(k_ref[j])) + _tile_lanes(pen_ref[j], BLK // LANES)
        m_new, l_new, a_new = _softmax_step(sj, rows(v_ref[j]), m_ref[...], l_ref[...], acc_ref[...])
        m_ref[...] = m_new
        l_ref[...] = l_new
        acc_ref[...] = a_new
        return carry

    lax.fori_loop(0, n, past, 0)

    o_ref[...] = _merge_heads(acc_ref[...] / l_ref[...]).reshape(o_ref.shape).astype(o_ref.dtype)


def _moba_attention(qkv, own_bias):
    B, P3, S, _ = qkv.shape
    P = P3 // 3
    pa = P // 2
    nb = S // BLK
    v6 = _seq_view(qkv)
    full = (None, None, nb, 16, 16, LANES)
    tile = (None, None, None, 16, 16, LANES)
    out = pl.pallas_call(
        _moba_kernel,
        grid=(B, pa, nb),
        in_specs=[
            pl.BlockSpec(tile, lambda b, p, n: (b, pa + p, n, 0, 0, 0)),
            pl.BlockSpec(full, lambda b, p, n: (b, P + pa + p, 0, 0, 0, 0)),
            pl.BlockSpec(full, lambda b, p, n: (b, 2 * P + pa + p, 0, 0, 0, 0)),
            pl.BlockSpec(own_bias.shape, lambda b, p, n: (0, 0)),
        ],
        out_specs=pl.BlockSpec(tile, lambda b, p, n: (b, p, n, 0, 0, 0)),
        out_shape=jax.ShapeDtypeStruct((B, pa, nb, 16, 16, LANES), jnp.bfloat16),
        scratch_shapes=[
            pltpu.VMEM((LANES, LANES), jnp.float32),
            pltpu.VMEM((nb, 2 * BLK, LANES), jnp.float32),
            pltpu.VMEM((2 * BLK, LANES), jnp.float32),
            pltpu.VMEM((2 * BLK, LANES), jnp.float32),
            pltpu.VMEM((2 * BLK, LANES), jnp.float32),
        ],
        compiler_params=_cparams(3),
        name="moba_attn",
    )(v6, v6, v6, own_bias)
    return out.reshape(B, pa, S, LANES)


def _post_kernel(oa_ref, ob_ref, x_ref, gate_a_ref, shift_f_ref, scale_f_ref, beta_ref, w_ref, wr_ref, br_ref,
                 x1_ref, h2_ref, te_ref, tg_ref, wbf_ref, y_ref):
    b = pl.program_id(0)
    i = pl.program_id(1)

    @pl.when((b == 0) & (i == 0))
    def _():
        _cast_rows(w_ref, wbf_ref)

    half = oa_ref.shape[0] * LANES
    for off, o_ref in ((0, oa_ref), (half, ob_ref)):
        ss = None
        for c in range(o_ref.shape[0]):
            o = o_ref[c].astype(jnp.float32)
            part = jnp.sum(o * o, axis=-1, keepdims=True)
            ss = part if ss is None else ss + part
        inv = lax.rsqrt(ss / half + NORM_EPS)
        for c in range(o_ref.shape[0]):
            lo = off + c * LANES
            y_ref[:, lo:lo + LANES] = (o_ref[c].astype(jnp.float32) * inv * beta_ref[:, lo:lo + LANES]).astype(y_ref.dtype)

    mixed = jnp.dot(y_ref[...], wbf_ref[...], preferred_element_type=jnp.float32)
    x1 = x_ref[...] + gate_a_ref[...] * mixed
    x1_ref[...] = x1
    h2 = _rms_mod(x1, scale_f_ref[...], shift_f_ref[...])
    h2_ref[...] = h2

    logits = jnp.dot(h2, wr_ref[...], preferred_element_type=jnp.float32, precision=lax.Precision.HIGHEST) + br_ref[...]
    lane = lax.broadcasted_iota(jnp.int32, logits.shape, 1).astype(jnp.float32)
    te = jnp.zeros(logits.shape, jnp.float32)
    tg = jnp.zeros(logits.shape, jnp.float32)
    work = logits
    top0 = None
    den = None
    for kk in range(TOP_K):
        mv = jnp.max(work, axis=1, keepdims=True)
        idx = jnp.min(jnp.where(work == mv, lane, float(LANES)), axis=1, keepdims=True)
        work = jnp.where(lane == idx, -jnp.inf, work)
        if kk == 0:
            top0 = mv
        e = jnp.exp(mv - top0)
        den = e if den is None else den + e
        te = jnp.where(lane == kk, idx, te)
        tg = jnp.where(lane == kk, e, tg)
    te_ref[...] = te.astype(jnp.int32)
    tg_ref[...] = tg / den


def _post_attention(oa, ob, x, mod6, beta, w_out, w_router, b_router, tm=256):
    B, S, D = x.shape
    pa = oa.shape[1]
    ne = w_router.shape[1]
    assert TOP_K <= ne <= LANES
    w_router = jnp.pad(w_router, ((0, 0), (0, LANES - ne)))
    b_router = jnp.pad(b_router, (0, LANES - ne), constant_values=-jnp.inf)
    ne = LANES
    row = lambda k: pl.BlockSpec((None, None, 1, D), lambda b, i, k=k: (b, k, 0, 0))
    tok = pl.BlockSpec((None, tm, D), lambda b, i: (b, i, 0))
    small = pl.BlockSpec((None, tm, LANES), lambda b, i: (b, i, 0))
    return pl.pallas_call(
        _post_kernel,
        grid=(B, S // tm),
        in_specs=[
            pl.BlockSpec((None, pa, tm, LANES), lambda b, i: (b, 0, i, 0)),
            pl.BlockSpec((None, pa, tm, LANES), lambda b, i: (b, 0, i, 0)),
            tok, row(2), row(3), row(4),
            pl.BlockSpec((1, D), lambda b, i: (0, 0)),
            pl.BlockSpec(w_out.shape, lambda b, i: (0, 0), pipeline_mode=pl.Buffered(1)),
            pl.BlockSpec(w_router.shape, lambda b, i: (0, 0)),
            pl.BlockSpec((1, ne), lambda b, i: (0, 0)),
        ],
        out_specs=[tok, tok, small, small],
        out_shape=[
            jax.ShapeDtypeStruct((B, S, D), jnp.float32),
            jax.ShapeDtypeStruct((B, S, D), jnp.float32),
            jax.ShapeDtypeStruct((B, S, LANES), jnp.int32),
            jax.ShapeDtypeStruct((B, S, LANES), jnp.float32),
        ],
        scratch_shapes=[pltpu.VMEM(w_out.shape, jnp.bfloat16), pltpu.VMEM((tm, D), jnp.bfloat16)],
        compiler_params=_cparams(2),
        name="post_attn_router",
    )(oa, ob, x, mod6, mod6, mod6, beta, w_out, w_router, b_router.reshape(1, ne))


def _route(top_e, top_g, n_experts):
    T = top_e.shape[0]
    A = T * TOP_K
    e = top_e.reshape(A)
    onehot = (e[:, None] == jnp.arange(n_experts, dtype=jnp.int32)[None, :]).astype(jnp.int32)
    csum = jnp.cumsum(onehot, axis=0)
    rank = jnp.take_along_axis(csum, e[:, None], axis=1)[:, 0] - 1
    counts = csum[-1]
    padded = ((counts + EXPERT_TILE - 1) // EXPERT_TILE) * EXPERT_TILE
    pend = jnp.cumsum(padded)
    pstart = pend - padded
    dest = (pstart[e] + rank).astype(jnp.int32)
    n_slots = A + n_experts * EXPERT_TILE
    n_blocks = n_slots // EXPERT_TILE
    tok = jnp.arange(A, dtype=jnp.int32) // TOP_K
    slot_tok = jnp.zeros((n_slots,), jnp.int32).at[dest].set(tok)
    slot_g = jnp.zeros((n_slots,), jnp.float32).at[dest].set(top_g.reshape(A))
    n_used = (pend[-1] // EXPERT_TILE).astype(jnp.int32)
    blk = jnp.arange(n_blocks, dtype=jnp.int32)
    be = jnp.sum((pend[None, :] <= (blk * EXPERT_TILE)[:, None]).astype(jnp.int32), axis=1)
    be = jnp.minimum(be, n_experts - 1)
    be = jnp.where(blk < n_used, be, be[jnp.maximum(n_used - 1, 0)])
    return dest.reshape(T, TOP_K), slot_tok, slot_g, be, n_used.reshape(1)


def _expert_kernel(be_ref, nused_ref, tok_ref, g_ref, h_hbm, wu_ref, bu_ref, wd_ref, bd_ref, o_ref,
                   xbuf, sem, wub, wdb):
    i = pl.program_id(0)
    tm = xbuf.shape[0]
    d_ff = wd_ref.shape[0]

    def row_copy(r):
        return pltpu.make_async_copy(h_hbm.at[pl.ds(tok_ref[0, r], 1)], xbuf.at[pl.ds(r, 1)], sem)

    @pl.when(i < nused_ref[0])
    def _():
        def start(r, c):
            row_copy(r).start()
            return c

        lax.fori_loop(0, tm, start, 0)

        @pl.when((i == 0) | (be_ref[i] != be_ref[jnp.maximum(i - 1, 0)]))
        def _():
            _cast_rows(wu_ref, wub)
            _cast_rows(wd_ref, wdb)

        def wait(r, c):
            row_copy(r).wait()
            return c

        lax.fori_loop(0, tm, wait, 0)

        x = xbuf[...].astype(jnp.bfloat16)
        u = jnp.dot(x, wub[...], preferred_element_type=jnp.float32) + bu_ref[...]
        glu = jnp.minimum(u[:, :d_ff], SWIGLU_LIMIT)
        lin = jnp.clip(u[:, d_ff:], -SWIGLU_LIMIT, SWIGLU_LIMIT)
        act = glu * (1.0 / (1.0 + jnp.exp(-SWIGLU_ALPHA * glu))) * (lin + 1.0)
        out = jnp.dot(act.astype(jnp.bfloat16), wdb[...], preferred_element_type=jnp.float32) + bd_ref[...]
        o_ref[...] = out * g_ref[...]

    @pl.when(i >= nused_ref[0])
    def _():
        o_ref[...] = jnp.zeros(o_ref.shape, o_ref.dtype)


def _experts(h2, slot_tok, slot_g, be, n_used, w_up, b_up, w_down, b_down):
    T, D = h2.shape
    ne, _, f2 = w_up.shape
    d_ff = w_down.shape[1]
    tm = EXPERT_TILE
    n_slots = slot_tok.shape[0]
    n_blocks = n_slots // tm
    grid_spec = pltpu.PrefetchScalarGridSpec(
        num_scalar_prefetch=2,
        grid=(n_blocks,),
        in_specs=[
            pl.BlockSpec((None, 1, tm), lambda i, be, nu: (i, 0, 0), memory_space=pltpu.SMEM),
            pl.BlockSpec((None, tm, 1), lambda i, be, nu: (i, 0, 0)),
            pl.BlockSpec(memory_space=pl.ANY),
            pl.BlockSpec((None, D, f2), lambda i, be, nu: (be[i], 0, 0)),
            pl.BlockSpec((None, 1, f2), lambda i, be, nu: (be[i], 0, 0)),
            pl.BlockSpec((None, d_ff, D), lambda i, be, nu: (be[i], 0, 0)),
            pl.BlockSpec((None, 1, D), lambda i, be, nu: (be[i], 0, 0)),
        ],
        out_specs=pl.BlockSpec((tm, D), lambda i, be, nu: (i, 0)),
        scratch_shapes=[
            pltpu.VMEM((tm, D), jnp.float32),
            pltpu.SemaphoreType.DMA(()),
            pltpu.VMEM((D, f2), jnp.bfloat16),
            pltpu.VMEM((d_ff, D), jnp.bfloat16),
        ],
    )
    return pl.pallas_call(
        _expert_kernel,
        grid_spec=grid_spec,
        out_shape=jax.ShapeDtypeStruct((n_slots, D), jnp.float32),
        compiler_params=_cparams(1),
        name="experts",
    )(be, n_used, slot_tok.reshape(n_blocks, 1, tm), slot_g.reshape(n_blocks, tm, 1), h2,
      w_up, b_up.reshape(ne, 1, f2), w_down, b_down.reshape(ne, 1, D))


def _combine_kernel(dest_ref, y_hbm, x_ref, gate_ref, o_ref, ybuf, sem):
    tm = x_ref.shape[0]

    def row_copy(r):
        return pltpu.make_async_copy(y_hbm.at[pl.ds(dest_ref[0, r], 1)], ybuf.at[pl.ds(r, 1)], sem)

    def start(r, c):
        row_copy(r).start()
        return c

    lax.fori_loop(0, TOP_K * tm, start, 0)

    def wait(r, c):
        row_copy(r).wait()
        return c

    lax.fori_loop(0, TOP_K * tm, wait, 0)
    y = ybuf[0:tm, :]
    for k in range(1, TOP_K):
        y = y + ybuf[k * tm:(k + 1) * tm, :]
    o_ref[...] = x_ref[...] + gate_ref[...] * y


def _combine(outs, dest, x1, mod6, tm=128):
    B, S, D = x1.shape
    nt = S // tm
    d = dest.reshape(B * nt, tm, TOP_K).transpose(0, 2, 1).reshape(B * nt, 1, TOP_K * tm)
    tok = pl.BlockSpec((None, tm, D), lambda b, i: (b, i, 0))
    return pl.pallas_call(
        _combine_kernel,
        grid=(B, nt),
        in_specs=[
            pl.BlockSpec((None, 1, TOP_K * tm), lambda b, i: (b * nt + i, 0, 0), memory_space=pltpu.SMEM),
            pl.BlockSpec(memory_space=pl.ANY),
            tok,
            pl.BlockSpec((None, None, 1, D), lambda b, i: (b, 5, 0, 0)),
        ],
        out_specs=tok,
        out_shape=jax.ShapeDtypeStruct((B, S, D), jnp.float32),
        scratch_shapes=[pltpu.VMEM((TOP_K * tm, D), jnp.float32), pltpu.SemaphoreType.DMA(())],
        compiler_params=_cparams(2),
        name="moe_combine",
    )(d, outs, x1, mod6)


def _final_kernel(x_ref, g_ref, o_ref):
    x = x_ref[...]
    ms = jnp.mean(x * x, axis=-1, keepdims=True)
    o_ref[...] = x * lax.rsqrt(ms + NORM_EPS) * g_ref[...]


def _final_norm(x, g_final, tm=512):
    B, S, D = x.shape
    tok = pl.BlockSpec((None, tm, D), lambda b, i: (b, i, 0))
    return pl.pallas_call(
        _final_kernel,
        grid=(B, S // tm),
        in_specs=[tok, pl.BlockSpec((1, D), lambda b, i: (0, 0))],
        out_specs=tok,
        out_shape=jax.ShapeDtypeStruct((B, S, D), jnp.float32),
        compiler_params=_cparams(2),
        name="final_norm",
    )(x, g_final.reshape(1, D))


def _rotary_tables(positions, scale_q):
    inv_freq = 1.0 / (ROPE_THETA ** (jnp.arange(0, HEAD_DIM, 2, dtype=jnp.float32) / HEAD_DIM))
    ang = positions[..., None].astype(jnp.float32) * inv_freq
    cos, sin = jnp.cos(ang), jnp.sin(ang)
    reps = LANES // HEAD_DIM
    cos_l = jnp.tile(jnp.concatenate([cos, cos], axis=-1), (1, 1, reps))
    sin_l = jnp.tile(jnp.concatenate([-sin, sin], axis=-1), (1, 1, reps))
    cos_t = jnp.stack([cos_l * scale_q, cos_l, jnp.ones_like(cos_l)])
    sin_t = jnp.stack([sin_l * scale_q, sin_l, jnp.zeros_like(sin_l)])
    return cos_t, sin_t


def kernel(x, c, positions, w_ada, b_ada, w_in, beta_a, beta_b, w_out, w_router, b_router, w_up, b_up, w_down, b_down, g_final):
    B, S, D = x.shape
    depth = w_ada.shape[0]
    ne = w_router.shape[-1]
    biases = tuple(jnp.asarray(b) for b in _dilated_biases())
    own_bias = jnp.asarray(_own_bias())

    xp = _to_perm(x)
    pos_p = _to_perm(positions)
    cos_t, sin_t = _rotary_tables(pos_p, HEAD_DIM ** -0.5)
    mod = _ada_mod(c, w_ada, b_ada)

    for l in range(depth):
        mod6 = mod[l, :B].reshape(B, 6, 1, D)
        qkv = _qkv_proj(xp, mod6, w_in[l], cos_t, sin_t)
        oa = _dilated_attention(qkv, biases)
        ob = _moba_attention(qkv, own_bias)
        beta = jnp.concatenate([beta_a[l], beta_b[l]]).reshape(1, D)
        x1, h2, te, tg = _post_attention(oa, ob, xp, mod6, beta, w_out[l], w_router[l], b_router[l])
        dest, slot_tok, slot_g, be, n_used = _route(te.reshape(B * S, LANES)[:, :TOP_K],
                                                    tg.reshape(B * S, LANES)[:, :TOP_K], ne)
        outs = _experts(h2.reshape(B * S, D), slot_tok, slot_g, be, n_used, w_up[l], b_up[l], w_down[l], b_down[l])
        xp = _combine(outs, dest, x1, mod6)

    return _from_perm(_final_norm(xp, g_final))
```

```python
import math

import numpy as np
import jax
import jax.numpy as jnp
from jax import lax
from jax.experimental import pallas as pl
from jax.experimental.pallas import tpu as pltpu

HEAD_DIM = 64
LANES = 128
BLK = 256
BAND = 128
DILATIONS = (1, 4, 16)
MOBA_TOPK = 3
TOP_K = 4
SWIGLU_LIMIT = 7.0
SWIGLU_ALPHA = 1.702
ROPE_THETA = 10000.0
NORM_EPS = 1e-6
NEG = -1e30
EXPERT_TILE = 256
VMEM_LIMIT = 60000 * 1024

_ARB = pltpu.ARBITRARY


def _cparams(n_axes):
    return pltpu.CompilerParams(dimension_semantics=(_ARB,) * n_axes, vmem_limit_bytes=VMEM_LIMIT)


def _to_perm(x):
    B, S = x.shape[:2]
    rest = x.shape[2:]
    x = x.reshape(B, S // BLK, 16, 4, 4, *rest)
    x = jnp.swapaxes(x, 2, 4)
    return x.reshape(B, S, *rest)


def _from_perm(x):
    B, S = x.shape[:2]
    rest = x.shape[2:]
    x = x.reshape(B, S // BLK, 4, 4, 16, *rest)
    x = jnp.swapaxes(x, 2, 4)
    return x.reshape(B, S, *rest)


def _local_pos():
    p = np.arange(BLK)
    return 16 * (p % 16) + 4 * ((p // 16) % 4) + p // 64


def _band_bias(seq_q, seq_k):
    dist = seq_q[:, None] - seq_k[None, :]
    ok = (dist >= 0) & (dist <= BAND)
    half = seq_k.shape[0] // 2
    exists = np.arange(seq_k.shape[0])[None, :] >= half
    normal = np.where(ok, 0.0, NEG).astype(np.float32)
    first = np.where(ok & exists, 0.0, NEG).astype(np.float32)
    two = np.stack([first, normal])
    return np.concatenate([two, two], axis=1)


def _dilated_biases():
    lp = _local_pos()
    b1 = _band_bias(lp + BLK, np.concatenate([lp, lp + BLK]))
    hi, c, lo = np.meshgrid(np.arange(4), np.arange(4), np.arange(16), indexing="ij")
    seq_k4 = (64 * hi + 4 * lo + c).reshape(-1)
    seq_q4 = seq_k4[:128] + 128
    b4 = _band_bias(seq_q4, seq_k4)
    b16 = _band_bias(np.arange(128) + 128, np.arange(256))
    return b1, b4, b16


def _own_bias():
    lp = _local_pos()
    ok = lp[None, :] <= lp[:, None]
    m = np.where(ok, 0.0, NEG).astype(np.float32)
    return np.concatenate([m, m], axis=0)


def _stack_heads(q):
    lane = lax.broadcasted_iota(jnp.int32, q.shape, 1)
    zero = jnp.zeros_like(q)
    return jnp.concatenate([jnp.where(lane < HEAD_DIM, q, zero), jnp.where(lane >= HEAD_DIM, q, zero)], axis=0)


def _scores(qs, kk):
    return lax.dot_general(qs, kk, (((1,), (1,)), ((), ())), preferred_element_type=jnp.float32)


def _tile_lanes(x, k):
    return x if k == 1 else jnp.concatenate([x] * k, axis=1)


def _softmax_step(s, vv, m_old, l_old, acc_old):
    m_cur = jnp.max(s, axis=1, keepdims=True)
    m_new = jnp.maximum(m_old, m_cur)
    alpha = jnp.exp(m_old - m_new)
    p = jnp.exp(s - _tile_lanes(m_new, s.shape[1] // LANES))
    l_new = alpha * l_old + jnp.sum(p, axis=1, keepdims=True)
    pv = jnp.dot(p.astype(vv.dtype), vv, preferred_element_type=jnp.float32)
    return m_new, l_new, alpha * acc_old + pv


def _merge_heads(x):
    n = x.shape[0] // 2
    lane = lax.broadcasted_iota(jnp.int32, (n, LANES), 1)
    return jnp.where(lane < HEAD_DIM, x[:n], x[n:])


def _ada_kernel(c_ref, w_ref, b_ref, o_ref):
    c = c_ref[...]
    sc = (c * (1.0 / (1.0 + jnp.exp(-c)))).astype(jnp.bfloat16)
    o_ref[...] = jnp.dot(sc, w_ref[...].astype(jnp.bfloat16), preferred_element_type=jnp.float32) + b_ref[...]


def _ada_mod(c, w_ada, b_ada):
    depth, d, n = w_ada.shape
    rows = 8
    tn = math.gcd(n, 1024)
    cp = jnp.zeros((rows, d), jnp.float32).at[: c.shape[0]].set(c)
    return pl.pallas_call(
        _ada_kernel,
        grid=(depth, n // tn),
        in_specs=[
            pl.BlockSpec((rows, d), lambda l, j: (0, 0)),
            pl.BlockSpec((None, d, tn), lambda l, j: (l, 0, j)),
            pl.BlockSpec((None, 1, tn), lambda l, j: (l, 0, j)),
        ],
        out_specs=pl.BlockSpec((None, rows, tn), lambda l, j: (l, 0, j)),
        out_shape=jax.ShapeDtypeStruct((depth, rows, n), jnp.float32),
        compiler_params=_cparams(2),
        name="ada_mod",
    )(cp, w_ada, b_ada.reshape(depth, 1, n))


def _rms_mod(x, scale, shift):
    ms = jnp.mean(x * x, axis=-1, keepdims=True)
    return x * lax.rsqrt(ms + NORM_EPS) * (1.0 + scale) + shift


def _cast_rows(src_ref, dst_ref):
    rows_per_step = math.gcd(src_ref.shape[0], 256)
    n = src_ref.shape[0] // rows_per_step

    def body(i, carry):
        r = pl.multiple_of(i * rows_per_step, rows_per_step)
        dst_ref[pl.ds(r, rows_per_step), :] = src_ref[pl.ds(r, rows_per_step), :].astype(dst_ref.dtype)
        return carry

    lax.fori_loop(0, n, body, 0)


def _qkv_kernel(x_ref, shift_ref, scale_ref, w_ref, cos_ref, sin_ref, o_ref, wbf_ref):
    b = pl.program_id(1)
    i = pl.program_id(2)

    @pl.when((b == 0) & (i == 0))
    def _():
        _cast_rows(w_ref, wbf_ref)

    h = _rms_mod(x_ref[...], scale_ref[...], shift_ref[...]).astype(jnp.bfloat16)
    acc = jnp.dot(h, wbf_ref[...], preferred_element_type=jnp.float32)
    cos = cos_ref[...]
    sin = sin_ref[...]
    lane = lax.broadcasted_iota(jnp.int32, cos.shape, 1)
    first_half = (lane % HEAD_DIM) < (HEAD_DIM // 2)
    for c in range(o_ref.shape[0]):
        a = acc[:, c * LANES:(c + 1) * LANES]
        partner = jnp.where(first_half, pltpu.roll(a, LANES - HEAD_DIM // 2, 1), pltpu.roll(a, HEAD_DIM // 2, 1))
        o_ref[c] = (a * cos + partner * sin).astype(o_ref.dtype)


def _qkv_proj(x, mod6, w_in, layer, cos_t, sin_t, tm=256):
    B, S, D = x.shape
    n3 = w_in.shape[2]
    width = n3 // 3
    pairs = width // LANES
    return pl.pallas_call(
        _qkv_kernel,
        grid=(3, B, S // tm),
        in_specs=[
            pl.BlockSpec((None, tm, D), lambda j, b, i: (b, i, 0)),
            pl.BlockSpec((None, None, 1, D), lambda j, b, i: (b, 0, 0, 0)),
            pl.BlockSpec((None, None, 1, D), lambda j, b, i: (b, 1, 0, 0)),
            pl.BlockSpec((None, D, width), lambda j, b, i: (layer, 0, j), pipeline_mode=pl.Buffered(1)),
            pl.BlockSpec((None, None, tm, LANES), lambda j, b, i: (j, b, i, 0)),
            pl.BlockSpec((None, None, tm, LANES), lambda j, b, i: (j, b, i, 0)),
        ],
        out_specs=pl.BlockSpec((None, pairs, tm, LANES), lambda j, b, i: (b, j, i, 0)),
        out_shape=jax.ShapeDtypeStruct((B, 3 * pairs, S, LANES), jnp.bfloat16),
        scratch_shapes=[pltpu.VMEM((D, width), jnp.bfloat16)],
        compiler_params=_cparams(3),
        name="qkv_proj",
    )(x, mod6, mod6, w_in, cos_t, sin_t)


def _dilated_kernel(q_ref, k_ref, v_ref, b1_ref, b4_ref, b16_ref, o_ref, m_ref, l_ref, acc_ref):
    nb = q_ref.shape[0]

    def rows(x):
        return x.reshape(-1, LANES)

    def update(q, kk, vv, bias, idx):
        qs = _stack_heads(q)
        s = _scores(qs, kk) + bias
        a_raw = acc_ref[idx]
        shp = a_raw.shape
        m_old = jnp.concatenate([rows(m_ref[(0,) + idx]), rows(m_ref[(1,) + idx])], axis=0)
        l_old = jnp.concatenate([rows(l_ref[(0,) + idx]), rows(l_ref[(1,) + idx])], axis=0)
        a_old = jnp.concatenate([rows(a_raw), rows(a_raw)], axis=0)
        m_new, l_new, a_new = _softmax_step(s, vv, m_old, l_old, a_old)
        n = q.shape[0]
        m_ref[(0,) + idx] = m_new[:n].reshape(shp)
        m_ref[(1,) + idx] = m_new[n:].reshape(shp)
        l_ref[(0,) + idx] = l_new[:n].reshape(shp)
        l_ref[(1,) + idx] = l_new[n:].reshape(shp)
        acc_ref[idx] = _merge_heads(a_new).reshape(shp)

    def init(n, carry):
        m_ref[0, n] = jnp.full(m_ref.shape[2:], NEG, jnp.float32)
        m_ref[1, n] = jnp.full(m_ref.shape[2:], NEG, jnp.float32)
        l_ref[0, n] = jnp.zeros(l_ref.shape[2:], jnp.float32)
        l_ref[1, n] = jnp.zeros(l_ref.shape[2:], jnp.float32)
        acc_ref[n] = jnp.zeros(acc_ref.shape[1:], jnp.float32)
        return carry

    lax.fori_loop(0, nb, init, 0)

    def d1(n, carry):
        prev = jnp.maximum(n - 1, 0)
        kk = jnp.concatenate([rows(k_ref[prev]), rows(k_ref[n])], axis=0)
        vv = jnp.concatenate([rows(v_ref[prev]), rows(v_ref[n])], axis=0)
        update(rows(q_ref[n]), kk, vv, b1_ref[jnp.minimum(n, 1)], (n,))
        return carry

    lax.fori_loop(0, nb, d1, 0)

    per4 = nb // 2

    def d4(t, carry):
        r4 = t // per4
        a = t % per4
        cur = pl.ds(2 * a, 2)
        prev = pl.ds(jnp.maximum(2 * a - 2, 0), 2)
        grp = pl.ds(4 * r4, 4)
        kk = jnp.concatenate([rows(k_ref[prev, grp]), rows(k_ref[cur, grp])], axis=0)
        vv = jnp.concatenate([rows(v_ref[prev, grp]), rows(v_ref[cur, grp])], axis=0)
        update(rows(q_ref[cur, grp]), kk, vv, b4_ref[jnp.minimum(a, 1)], (cur, grp))
        return carry

    lax.fori_loop(0, 4 * per4, d4, 0)

    per16 = nb // 8

    def d16(t, carry):
        rr = t // per16
        b = t % per16
        cur = pl.ds(8 * b, 8)
        prev = pl.ds(jnp.maximum(8 * b - 8, 0), 8)
        kk = jnp.concatenate([rows(k_ref[prev, rr]), rows(k_ref[cur, rr])], axis=0)
        vv = jnp.concatenate([rows(v_ref[prev, rr]), rows(v_ref[cur, rr])], axis=0)
        update(rows(q_ref[cur, rr]), kk, vv, b16_ref[jnp.minimum(b, 1)], (cur, rr))
        return carry

    lax.fori_loop(0, 16 * per16, d16, 0)

    def fin(n, carry):
        lane = lax.broadcasted_iota(jnp.int32, acc_ref.shape[1:], 2)
        den = jnp.where(lane < HEAD_DIM, l_ref[0, n], l_ref[1, n])
        o_ref[n] = (acc_ref[n] / den).astype(o_ref.dtype)
        return carry

    lax.fori_loop(0, nb, fin, 0)


def _seq_view(a):
    B, P, S, L = a.shape
    return a.reshape(B, P, S // BLK, 16, 16, L)


def _dilated_attention(qkv, biases):
    B, P3, S, _ = qkv.shape
    P = P3 // 3
    pa = P // 2
    nb = S // BLK
    assert S % (16 * BAND) == 0, "sequence must be a whole number of dilation-16 band blocks"
    v6 = _seq_view(qkv)
    blk = (None, None, nb, 16, 16, LANES)
    b1, b4, b16 = biases
    out = pl.pallas_call(
        _dilated_kernel,
        grid=(B, pa),
        in_specs=[
            pl.BlockSpec(blk, lambda b, p: (b, p, 0, 0, 0, 0)),
            pl.BlockSpec(blk, lambda b, p: (b, P + p, 0, 0, 0, 0)),
            pl.BlockSpec(blk, lambda b, p: (b, 2 * P + p, 0, 0, 0, 0)),
            pl.BlockSpec(b1.shape, lambda b, p: (0, 0, 0)),
            pl.BlockSpec(b4.shape, lambda b, p: (0, 0, 0)),
            pl.BlockSpec(b16.shape, lambda b, p: (0, 0, 0)),
        ],
        out_specs=pl.BlockSpec(blk, lambda b, p: (b, p, 0, 0, 0, 0)),
        out_shape=jax.ShapeDtypeStruct((B, pa, nb, 16, 16, LANES), jnp.bfloat16),
        scratch_shapes=[
            pltpu.VMEM((2, nb, 16, 16, LANES), jnp.float32),
            pltpu.VMEM((2, nb, 16, 16, LANES), jnp.float32),
            pltpu.VMEM((nb, 16, 16, LANES), jnp.float32),
        ],
        compiler_params=_cparams(2),
        name="dilated_attn",
    )(v6, v6, v6, b1, b4, b16)
    return out.reshape(B, pa, S, LANES)


def _moba_kernel(q_ref, k_ref, v_ref, own_ref, hot_ref, o_ref, kmean_ref, qa_ref, sa_ref, sb_ref, m_ref, l_ref, acc_ref):
    n = pl.program_id(2)
    nb = k_ref.shape[0]
    nbp = -(-nb // 8) * 8

    def rows(x):
        return x.reshape(-1, LANES)

    @pl.when(n == 0)
    def _():
        kmean_ref[...] = jnp.zeros(kmean_ref.shape, jnp.float32)
        for j in range(nb):
            kmean_ref[pl.ds(j, 1), :] = jnp.mean(rows(k_ref[j]).astype(jnp.float32), axis=0, keepdims=True)

    qs = _stack_heads(rows(q_ref[...]))
    nq = qs.shape[0]

    gate = _scores(kmean_ref[...].astype(jnp.bfloat16), qs)[:nbp]
    jidx = lax.broadcasted_iota(jnp.int32, gate.shape, 0)
    g = jnp.where(jidx < n, gate, -jnp.inf)
    rank = jnp.zeros(gate.shape, jnp.int32)
    for jp in range(nb):
        other = g[jp:jp + 1, :]
        beats = (other > g) | ((other == g) & (jp < jidx))
        rank = rank + beats.astype(jnp.int32)
    sel = (rank < MOBA_TOPK) & (jidx < n)
    pen_t = jnp.where(sel, 0.0, NEG).astype(jnp.float32)
    pen_t = jnp.concatenate([pen_t, jnp.full((LANES - nbp, nq), NEG, jnp.float32)], axis=0)
    qa_ref[...] = jnp.concatenate([qs, pen_t.T.astype(qs.dtype)], axis=1)

    s = _scores(qs, rows(k_ref[n])) + own_ref[...]
    m0 = jnp.max(s, axis=1, keepdims=True)
    p = jnp.exp(s - m0)
    m_ref[...] = jnp.broadcast_to(m0, m_ref.shape)
    l_ref[...] = jnp.broadcast_to(jnp.sum(p, axis=1, keepdims=True), l_ref.shape)
    acc_ref[...] = jnp.dot(p.astype(jnp.bfloat16), rows(v_ref[n]), preferred_element_type=jnp.float32)

    def pair_scores(t):
        j0 = jnp.minimum(2 * t, nb - 2)
        j1 = j0 + 1
        kk = jnp.concatenate([jnp.concatenate([rows(k_ref[j0]), hot_ref[j0]], axis=1),
                              jnp.concatenate([rows(k_ref[j1]), hot_ref[j1]], axis=1)], axis=0)
        return _scores(qa_ref[...], kk)

    def absorb(s, t):
        j0 = 2 * t
        vv = jnp.concatenate([rows(v_ref[j0]), rows(v_ref[j0 + 1])], axis=0)
        m_new, l_new, a_new = _softmax_step(s, vv, m_ref[...], l_ref[...], acc_ref[...])
        m_ref[...] = m_new
        l_ref[...] = l_new
        acc_ref[...] = a_new

    sa_ref[...] = pair_scores(0)

    def past(u, carry):
        t = 2 * u
        sb_ref[...] = pair_scores(t + 1)
        absorb(sa_ref[...], t)
        sa_ref[...] = pair_scores(t + 2)
        absorb(sb_ref[...], t + 1)
        return carry

    n_pairs = (n + 1) // 2
    lax.fori_loop(0, (n_pairs + 1) // 2, past, 0)

    o_ref[...] = _merge_heads(acc_ref[...] / l_ref[...]).reshape(o_ref.shape).astype(o_ref.dtype)


def _block_one_hot(nb):
    hot = np.zeros((nb, BLK, LANES), np.float32)
    hot[np.arange(nb), :, np.arange(nb)] = 1.0
    return jnp.asarray(hot, jnp.bfloat16)


def _moba_attention(qkv, own_bias):
    B, P3, S, _ = qkv.shape
    P = P3 // 3
    pa = P // 2
    nb = S // BLK
    assert nb <= LANES
    v6 = _seq_view(qkv)
    hot = _block_one_hot(nb)
    full = (None, None, nb, 16, 16, LANES)
    tile = (None, None, None, 16, 16, LANES)
    out = pl.pallas_call(
        _moba_kernel,
        grid=(B, pa, nb),
        in_specs=[
            pl.BlockSpec(tile, lambda b, p, n: (b, pa + p, n, 0, 0, 0)),
            pl.BlockSpec(full, lambda b, p, n: (b, P + pa + p, 0, 0, 0, 0)),
            pl.BlockSpec(full, lambda b, p, n: (b, 2 * P + pa + p, 0, 0, 0, 0)),
            pl.BlockSpec(own_bias.shape, lambda b, p, n: (0, 0)),
            pl.BlockSpec(hot.shape, lambda b, p, n: (0, 0, 0)),
        ],
        out_specs=pl.BlockSpec(tile, lambda b, p, n: (b, p, n, 0, 0, 0)),
        out_shape=jax.ShapeDtypeStruct((B, pa, nb, 16, 16, LANES), jnp.bfloat16),
        scratch_shapes=[
            pltpu.VMEM((LANES, LANES), jnp.float32),
            pltpu.VMEM((2 * BLK, 2 * LANES), jnp.bfloat16),
            pltpu.VMEM((2 * BLK, 2 * BLK), jnp.float32),
            pltpu.VMEM((2 * BLK, 2 * BLK), jnp.float32),
            pltpu.VMEM((2 * BLK, LANES), jnp.float32),
            pltpu.VMEM((2 * BLK, LANES), jnp.float32),
            pltpu.VMEM((2 * BLK, LANES), jnp.float32),
        ],
        compiler_params=_cparams(3),
        name="moba_attn",
    )(v6, v6, v6, own_bias, hot)
    return out.reshape(B, pa, S, LANES)


def _post_kernel(oa_ref, ob_ref, x_ref, gate_a_ref, shift_f_ref, scale_f_ref, beta_ref, w_ref, wr_ref, br_ref,
                 x1_ref, h2_ref, te_ref, tg_ref, wbf_ref, y_ref):
    b = pl.program_id(0)
    i = pl.program_id(1)

    @pl.when((b == 0) & (i == 0))
    def _():
        _cast_rows(w_ref, wbf_ref)

    half = oa_ref.shape[0] * LANES
    for off, o_ref in ((0, oa_ref), (half, ob_ref)):
        ss = None
        for c in range(o_ref.shape[0]):
            o = o_ref[c].astype(jnp.float32)
            part = jnp.sum(o * o, axis=-1, keepdims=True)
            ss = part if ss is None else ss + part
        inv = lax.rsqrt(ss / half + NORM_EPS)
        for c in range(o_ref.shape[0]):
            lo = off + c * LANES
            y_ref[:, lo:lo + LANES] = (o_ref[c].astype(jnp.float32) * inv * beta_ref[:, lo:lo + LANES]).astype(y_ref.dtype)

    mixed = jnp.dot(y_ref[...], wbf_ref[...], preferred_element_type=jnp.float32)
    x1 = x_ref[...] + gate_a_ref[...] * mixed
    x1_ref[...] = x1
    h2 = _rms_mod(x1, scale_f_ref[...], shift_f_ref[...])
    h2_ref[...] = h2

    logits = jnp.dot(h2, wr_ref[...], preferred_element_type=jnp.float32, precision=lax.Precision.HIGHEST) + br_ref[...]
    lane = lax.broadcasted_iota(jnp.int32, logits.shape, 1).astype(jnp.float32)
    te = jnp.zeros(logits.shape, jnp.float32)
    tg = jnp.zeros(logits.shape, jnp.float32)
    work = logits
    top0 = None
    den = None
    for kk in range(TOP_K):
        mv = jnp.max(work, axis=1, keepdims=True)
        idx = jnp.min(jnp.where(work == mv, lane, float(LANES)), axis=1, keepdims=True)
        work = jnp.where(lane == idx, -jnp.inf, work)
        if kk == 0:
            top0 = mv
        e = jnp.exp(mv - top0)
        den = e if den is None else den + e
        te = jnp.where(lane == kk, idx, te)
        tg = jnp.where(lane == kk, e, tg)
    te_ref[...] = te.astype(jnp.int32)
    tg_ref[...] = tg / den


def _post_attention(oa, ob, x, mod6, beta, w_out, layer, w_router, b_router, tm=256):
    B, S, D = x.shape
    pa = oa.shape[1]
    ne = w_router.shape[1]
    assert TOP_K <= ne <= LANES
    w_router = jnp.pad(w_router, ((0, 0), (0, LANES - ne)))
    b_router = jnp.pad(b_router, (0, LANES - ne), constant_values=-jnp.inf)
    ne = LANES
    row = lambda k: pl.BlockSpec((None, None, 1, D), lambda b, i, k=k: (b, k, 0, 0))
    tok = pl.BlockSpec((None, tm, D), lambda b, i: (b, i, 0))
    small = pl.BlockSpec((None, tm, LANES), lambda b, i: (b, i, 0))
    return pl.pallas_call(
        _post_kernel,
        grid=(B, S // tm),
        in_specs=[
            pl.BlockSpec((None, pa, tm, LANES), lambda b, i: (b, 0, i, 0)),
            pl.BlockSpec((None, pa, tm, LANES), lambda b, i: (b, 0, i, 0)),
            tok, row(2), row(3), row(4),
            pl.BlockSpec((1, D), lambda b, i: (0, 0)),
            pl.BlockSpec((None,) + w_out.shape[1:], lambda b, i: (layer, 0, 0), pipeline_mode=pl.Buffered(1)),
            pl.BlockSpec(w_router.shape, lambda b, i: (0, 0)),
            pl.BlockSpec((1, ne), lambda b, i: (0, 0)),
        ],
        out_specs=[tok, tok, small, small],
        out_shape=[
            jax.ShapeDtypeStruct((B, S, D), jnp.float32),
            jax.ShapeDtypeStruct((B, S, D), jnp.float32),
            jax.ShapeDtypeStruct((B, S, LANES), jnp.int32),
            jax.ShapeDtypeStruct((B, S, LANES), jnp.float32),
        ],
        scratch_shapes=[pltpu.VMEM(w_out.shape[1:], jnp.bfloat16), pltpu.VMEM((tm, D), jnp.bfloat16)],
        compiler_params=_cparams(2),
        name="post_attn_router",
    )(oa, ob, x, mod6, mod6, mod6, beta, w_out, w_router, b_router.reshape(1, ne))


def _route(top_e, n_experts):
    T = top_e.shape[0]
    A = T * TOP_K
    e = top_e.reshape(A)
    onehot = (e[:, None] == jnp.arange(n_experts, dtype=jnp.int32)[None, :]).astype(jnp.int32)
    csum = jnp.cumsum(onehot, axis=0)
    rank = jnp.take_along_axis(csum, e[:, None], axis=1)[:, 0] - 1
    counts = csum[-1]
    padded = ((counts + EXPERT_TILE - 1) // EXPERT_TILE) * EXPERT_TILE
    pend = jnp.cumsum(padded)
    pstart = pend - padded
    dest = (pstart[e] + rank).astype(jnp.int32)
    n_slots = A + n_experts * EXPERT_TILE
    n_blocks = n_slots // EXPERT_TILE
    n_used = (pend[-1] // EXPERT_TILE).astype(jnp.int32)
    blk = jnp.arange(n_blocks, dtype=jnp.int32)
    be = jnp.sum((pend[None, :] <= (blk * EXPERT_TILE)[:, None]).astype(jnp.int32), axis=1)
    be = jnp.minimum(be, n_experts - 1)
    be = jnp.where(blk < n_used, be, be[jnp.maximum(n_used - 1, 0)])
    return dest.reshape(T, TOP_K), pend.astype(jnp.int32), be, n_used.reshape(1)


def _tile_major(dest, tm):
    T = dest.shape[0]
    return dest.reshape(T // tm, tm, TOP_K).transpose(0, 2, 1).reshape(T // tm, 1, TOP_K * tm)


def _issue_pairs(n, make_copy, wait=False):
    def body(u, c):
        for par in range(2):
            cp = make_copy(2 * u + par)
            if wait:
                cp.wait()
            else:
                cp.start(priority=par)
        return c

    lax.fori_loop(0, n // 2, body, 0, unroll=4)


def _dispatch_kernel(pend_ref, dest_ref, h_hbm, xs_hbm, zbuf, sem):
    i = pl.program_id(0)
    tm = dest_ref.shape[1] // TOP_K
    tile = zbuf.shape[0]
    ne = pend_ref.shape[0]

    @pl.when(i == 0)
    def _():
        zbuf[...] = jnp.zeros(zbuf.shape, zbuf.dtype)

        def fill(e):
            end = pend_ref[e]
            start = jnp.where(e == 0, 0, pend_ref[jnp.maximum(e - 1, 0)])
            last = pl.multiple_of(jnp.maximum(end - tile, 0), tile)
            return end > start, pltpu.make_async_copy(zbuf, xs_hbm.at[pl.ds(last, tile)], sem)

        def start(e, c):
            ok, cp = fill(e)

            @pl.when(ok)
            def _():
                cp.start()
            return c

        def wait(e, c):
            ok, cp = fill(e)

            @pl.when(ok)
            def _():
                cp.wait()
            return c

        lax.fori_loop(0, ne, start, 0)
        lax.fori_loop(0, ne, wait, 0)

        def tail(b):
            off = pl.multiple_of(b * tile, tile)
            return off >= pend_ref[ne - 1], pltpu.make_async_copy(zbuf, xs_hbm.at[pl.ds(off, tile)], sem)

        def tail_start(b, c):
            ok, cp = tail(b)

            @pl.when(ok)
            def _():
                cp.start()
            return c

        def tail_wait(b, c):
            ok, cp = tail(b)

            @pl.when(ok)
            def _():
                cp.wait()
            return c

        n_tiles = xs_hbm.shape[0] // tile
        lax.fori_loop(0, n_tiles, tail_start, 0)
        lax.fori_loop(0, n_tiles, tail_wait, 0)

    def row_copy(r):
        return pltpu.make_async_copy(h_hbm.at[pl.ds(i * tm + r % tm, 1)], xs_hbm.at[pl.ds(dest_ref[0, r], 1)], sem)

    _issue_pairs(TOP_K * tm, row_copy)
    _issue_pairs(TOP_K * tm, row_copy, wait=True)


def _dispatch(h2, dest, pend, n_slots, tm=128):
    T, D = h2.shape
    grid_spec = pltpu.PrefetchScalarGridSpec(
        num_scalar_prefetch=1,
        grid=(T // tm,),
        in_specs=[
            pl.BlockSpec((None, 1, TOP_K * tm), lambda i, pe: (i, 0, 0), memory_space=pltpu.SMEM),
            pl.BlockSpec(memory_space=pl.ANY),
        ],
        out_specs=pl.BlockSpec(memory_space=pl.ANY),
        scratch_shapes=[pltpu.VMEM((EXPERT_TILE, D), h2.dtype), pltpu.SemaphoreType.DMA(())],
    )
    return pl.pallas_call(
        _dispatch_kernel,
        grid_spec=grid_spec,
        out_shape=jax.ShapeDtypeStruct((n_slots, D), h2.dtype),
        compiler_params=_cparams(1),
        name="moe_dispatch",
    )(pend, _tile_major(dest, tm), h2)


def _expert_kernel(be_ref, nused_ref, x_ref, wu_ref, bu_ref, wd_ref, bd_ref, o_ref, wub, wdb):
    i = pl.program_id(0)
    d_ff = wd_ref.shape[0]

    @pl.when(i < nused_ref[0])
    def _():
        @pl.when((i == 0) | (be_ref[i] != be_ref[jnp.maximum(i - 1, 0)]))
        def _():
            _cast_rows(wu_ref, wub)
            _cast_rows(wd_ref, wdb)

        x = x_ref[...].astype(jnp.bfloat16)
        u = jnp.dot(x, wub[...], preferred_element_type=jnp.float32) + bu_ref[...]
        glu = jnp.minimum(u[:, :d_ff], SWIGLU_LIMIT)
        lin = jnp.clip(u[:, d_ff:], -SWIGLU_LIMIT, SWIGLU_LIMIT)
        act = glu * (1.0 / (1.0 + jnp.exp(-SWIGLU_ALPHA * glu))) * (lin + 1.0)
        o_ref[...] = jnp.dot(act.astype(jnp.bfloat16), wdb[...], preferred_element_type=jnp.float32) + bd_ref[...]

    @pl.when(i >= nused_ref[0])
    def _():
        o_ref[...] = jnp.zeros(o_ref.shape, o_ref.dtype)


def _experts(xs, be, n_used, layer, w_up, b_up, w_down, b_down):
    n_slots, D = xs.shape
    depth, ne, _, f2 = w_up.shape
    d_ff = w_down.shape[2]
    tm = EXPERT_TILE
    n_blocks = n_slots // tm
    grid_spec = pltpu.PrefetchScalarGridSpec(
        num_scalar_prefetch=2,
        grid=(n_blocks,),
        in_specs=[
            pl.BlockSpec((tm, D), lambda i, be, nu: (jnp.minimum(i, nu[0] - 1), 0)),
            pl.BlockSpec((None, None, D, f2), lambda i, be, nu: (layer, be[i], 0, 0)),
            pl.BlockSpec((None, None, 1, f2), lambda i, be, nu: (layer, be[i], 0, 0)),
            pl.BlockSpec((None, None, d_ff, D), lambda i, be, nu: (layer, be[i], 0, 0)),
            pl.BlockSpec((None, None, 1, D), lambda i, be, nu: (layer, be[i], 0, 0)),
        ],
        out_specs=pl.BlockSpec((tm, D), lambda i, be, nu: (i, 0)),
        scratch_shapes=[
            pltpu.VMEM((D, f2), jnp.bfloat16),
            pltpu.VMEM((d_ff, D), jnp.bfloat16),
        ],
    )
    return pl.pallas_call(
        _expert_kernel,
        grid_spec=grid_spec,
        out_shape=jax.ShapeDtypeStruct((n_slots, D), jnp.float32),
        compiler_params=_cparams(1),
        name="experts",
    )(be, n_used, xs, w_up, b_up.reshape(depth, ne, 1, f2), w_down, b_down.reshape(depth, ne, 1, D))


def _combine_kernel(dest_ref, y_hbm, x_ref, gate_ref, tg_ref, o_ref, ybuf, sem):
    tm = x_ref.shape[0]

    def row_copy(r):
        return pltpu.make_async_copy(y_hbm.at[pl.ds(dest_ref[0, r], 1)], ybuf.at[pl.ds(r, 1)], sem)

    _issue_pairs(TOP_K * tm, row_copy)
    _issue_pairs(TOP_K * tm, row_copy, wait=True)
    tg = tg_ref[...]
    y = None
    for k in range(TOP_K):
        part = tg[:, k:k + 1] * ybuf[k * tm:(k + 1) * tm, :]
        y = part if y is None else y + part
    o_ref[...] = x_ref[...] + gate_ref[...] * y


def _combine(outs, dest, tg, x1, mod6, tm=128):
    B, S, D = x1.shape
    nt = S // tm
    tok = pl.BlockSpec((None, tm, D), lambda b, i: (b, i, 0))
    return pl.pallas_call(
        _combine_kernel,
        grid=(B, nt),
        in_specs=[
            pl.BlockSpec((None, 1, TOP_K * tm), lambda b, i: (b * nt + i, 0, 0), memory_space=pltpu.SMEM),
            pl.BlockSpec(memory_space=pl.ANY),
            tok,
            pl.BlockSpec((None, None, 1, D), lambda b, i: (b, 5, 0, 0)),
            pl.BlockSpec((None, tm, LANES), lambda b, i: (b, i, 0)),
        ],
        out_specs=tok,
        out_shape=jax.ShapeDtypeStruct((B, S, D), jnp.float32),
        scratch_shapes=[pltpu.VMEM((TOP_K * tm, D), jnp.float32), pltpu.SemaphoreType.DMA(())],
        compiler_params=_cparams(2),
        name="moe_combine",
    )(_tile_major(dest, tm), outs, x1, mod6, tg)


def _final_kernel(x_ref, g_ref, o_ref):
    x = x_ref[...]
    ms = jnp.mean(x * x, axis=-1, keepdims=True)
    o_ref[...] = x * lax.rsqrt(ms + NORM_EPS) * g_ref[...]


def _final_norm(x, g_final, tm=512):
    B, S, D = x.shape
    tok = pl.BlockSpec((None, tm, D), lambda b, i: (b, i, 0))
    return pl.pallas_call(
        _final_kernel,
        grid=(B, S // tm),
        in_specs=[tok, pl.BlockSpec((1, D), lambda b, i: (0, 0))],
        out_specs=tok,
        out_shape=jax.ShapeDtypeStruct((B, S, D), jnp.float32),
        compiler_params=_cparams(2),
        name="final_norm",
    )(x, g_final.reshape(1, D))


def _rotary_tables(positions, scale_q):
    inv_freq = 1.0 / (ROPE_THETA ** (jnp.arange(0, HEAD_DIM, 2, dtype=jnp.float32) / HEAD_DIM))
    ang = positions[..., None].astype(jnp.float32) * inv_freq
    cos, sin = jnp.cos(ang), jnp.sin(ang)
    reps = LANES // HEAD_DIM
    cos_l = jnp.tile(jnp.concatenate([cos, cos], axis=-1), (1, 1, reps))
    sin_l = jnp.tile(jnp.concatenate([-sin, sin], axis=-1), (1, 1, reps))
    cos_t = jnp.stack([cos_l * scale_q, cos_l, jnp.ones_like(cos_l)])
    sin_t = jnp.stack([sin_l * scale_q, sin_l, jnp.zeros_like(sin_l)])
    return cos_t, sin_t


def kernel(x, c, positions, w_ada, b_ada, w_in, beta_a, beta_b, w_out, w_router, b_router, w_up, b_up, w_down, b_down, g_final):
    B, S, D = x.shape
    depth = w_ada.shape[0]
    ne = w_router.shape[-1]
    biases = tuple(jnp.asarray(b) for b in _dilated_biases())
    own_bias = jnp.asarray(_own_bias())

    xp = _to_perm(x)
    pos_p = _to_perm(positions)
    cos_t, sin_t = _rotary_tables(pos_p, HEAD_DIM ** -0.5)
    mod = _ada_mod(c, w_ada, b_ada)

    for l in range(depth):
        mod6 = mod[l, :B].reshape(B, 6, 1, D)
        qkv = _qkv_proj(xp, mod6, w_in, l, cos_t, sin_t)
        oa = _dilated_attention(qkv, biases)
        ob = _moba_attention(qkv, own_bias)
        beta = jnp.concatenate([beta_a[l], beta_b[l]]).reshape(1, D)
        x1, h2, te, tg = _post_attention(oa, ob, xp, mod6, beta, w_out, l, w_router[l], b_router[l])
        dest, pend, be, n_used = _route(te.reshape(B * S, LANES)[:, :TOP_K], ne)
        xs = _dispatch(h2.reshape(B * S, D), dest, pend, be.shape[0] * EXPERT_TILE)
        outs = _experts(xs, be, n_used, l, w_up, b_up, w_down, b_down)
        xp = _combine(outs, dest, tg, x1, mod6)

    return _from_perm(_final_norm(xp, g_final))
```

```python
import math

import numpy as np
import jax
import jax.numpy as jnp
from jax import lax
from jax.experimental import pallas as pl
from jax.experimental.pallas import tpu as pltpu

HEAD_DIM = 64
LANES = 128
BLK = 256
BAND = 128
DILATIONS = (1, 4, 16)
MOBA_TOPK = 3
TOP_K = 4
SWIGLU_LIMIT = 7.0
SWIGLU_ALPHA = 1.702
ROPE_THETA = 10000.0
NORM_EPS = 1e-6
NEG = -1e30
EXPERT_TILE = 256
VMEM_LIMIT = 60000 * 1024

_ARB = pltpu.ARBITRARY


def _cparams(n_axes):
    return pltpu.CompilerParams(dimension_semantics=(_ARB,) * n_axes, vmem_limit_bytes=VMEM_LIMIT)


def _to_perm(x):
    B, S = x.shape[:2]
    rest = x.shape[2:]
    x = x.reshape(B, S // BLK, 16, 4, 4, *rest)
    x = jnp.swapaxes(x, 2, 4)
    return x.reshape(B, S, *rest)


def _from_perm(x):
    B, S = x.shape[:2]
    rest = x.shape[2:]
    x = x.reshape(B, S // BLK, 4, 4, 16, *rest)
    x = jnp.swapaxes(x, 2, 4)
    return x.reshape(B, S, *rest)


def _local_pos():
    p = np.arange(BLK)
    return 16 * (p % 16) + 4 * ((p // 16) % 4) + p // 64


def _band_bias(seq_q, seq_k):
    dist = seq_q[:, None] - seq_k[None, :]
    ok = (dist >= 0) & (dist <= BAND)
    half = seq_k.shape[0] // 2
    exists = np.arange(seq_k.shape[0])[None, :] >= half
    normal = np.where(ok, 0.0, NEG).astype(np.float32)
    first = np.where(ok & exists, 0.0, NEG).astype(np.float32)
    two = np.stack([first, normal])
    return np.concatenate([two, two], axis=1)


def _dilated_biases():
    lp = _local_pos()
    b1 = _band_bias(lp + BLK, np.concatenate([lp, lp + BLK]))
    hi, c, lo = np.meshgrid(np.arange(4), np.arange(4), np.arange(16), indexing="ij")
    seq_k4 = (64 * hi + 4 * lo + c).reshape(-1)
    seq_q4 = seq_k4[:128] + 128
    b4 = _band_bias(seq_q4, seq_k4)
    b16 = _band_bias(np.arange(128) + 128, np.arange(256))
    return b1, b4, b16


def _own_bias():
    lp = _local_pos()
    ok = lp[None, :] <= lp[:, None]
    m = np.where(ok, 0.0, NEG).astype(np.float32)
    return np.concatenate([m, m], axis=0)


def _stack_heads(q):
    lane = lax.broadcasted_iota(jnp.int32, q.shape, 1)
    zero = jnp.zeros_like(q)
    return jnp.concatenate([jnp.where(lane < HEAD_DIM, q, zero), jnp.where(lane >= HEAD_DIM, q, zero)], axis=0)


def _scores(qs, kk):
    return lax.dot_general(qs, kk, (((1,), (1,)), ((), ())), preferred_element_type=jnp.float32)


def _tile_lanes(x, k):
    return x if k == 1 else jnp.concatenate([x] * k, axis=1)


def _softmax_step(s, vv, m_old, l_old, acc_old):
    m_cur = jnp.max(s, axis=1, keepdims=True)
    m_new = jnp.maximum(m_old, m_cur)
    alpha = jnp.exp(m_old - m_new)
    p = jnp.exp(s - _tile_lanes(m_new, s.shape[1] // LANES))
    l_new = alpha * l_old + jnp.sum(p, axis=1, keepdims=True)
    pv = jnp.dot(p.astype(vv.dtype), vv, preferred_element_type=jnp.float32)
    return m_new, l_new, alpha * acc_old + pv


def _merge_heads(x):
    n = x.shape[0] // 2
    lane = lax.broadcasted_iota(jnp.int32, (n, LANES), 1)
    return jnp.where(lane < HEAD_DIM, x[:n], x[n:])


def _ada_kernel(c_ref, w_ref, b_ref, o_ref):
    c = c_ref[...]
    sc = (c * (1.0 / (1.0 + jnp.exp(-c)))).astype(jnp.bfloat16)
    o_ref[...] = jnp.dot(sc, w_ref[...].astype(jnp.bfloat16), preferred_element_type=jnp.float32) + b_ref[...]


def _ada_mod(c, w_ada, b_ada):
    depth, d, n = w_ada.shape
    rows = 8
    tn = math.gcd(n, 1024)
    cp = jnp.zeros((rows, d), jnp.float32).at[: c.shape[0]].set(c)
    return pl.pallas_call(
        _ada_kernel,
        grid=(depth, n // tn),
        in_specs=[
            pl.BlockSpec((rows, d), lambda l, j: (0, 0)),
            pl.BlockSpec((None, d, tn), lambda l, j: (l, 0, j)),
            pl.BlockSpec((None, 1, tn), lambda l, j: (l, 0, j)),
        ],
        out_specs=pl.BlockSpec((None, rows, tn), lambda l, j: (l, 0, j)),
        out_shape=jax.ShapeDtypeStruct((depth, rows, n), jnp.float32),
        compiler_params=_cparams(2),
        name="ada_mod",
    )(cp, w_ada, b_ada.reshape(depth, 1, n))


def _rms_mod(x, scale, shift):
    ms = jnp.mean(x * x, axis=-1, keepdims=True)
    return x * lax.rsqrt(ms + NORM_EPS) * (1.0 + scale) + shift


def _cast_rows(src_ref, dst_ref):
    rows_per_step = math.gcd(src_ref.shape[0], 256)
    n = src_ref.shape[0] // rows_per_step

    def body(i, carry):
        r = pl.multiple_of(i * rows_per_step, rows_per_step)
        dst_ref[pl.ds(r, rows_per_step), :] = src_ref[pl.ds(r, rows_per_step), :].astype(dst_ref.dtype)
        return carry

    lax.fori_loop(0, n, body, 0)


def _qkv_kernel(x_ref, shift_ref, scale_ref, w_ref, cos_ref, sin_ref, o_ref, wbf_ref):
    b = pl.program_id(1)
    i = pl.program_id(2)

    @pl.when((b == 0) & (i == 0))
    def _():
        _cast_rows(w_ref, wbf_ref)

    h = _rms_mod(x_ref[...], scale_ref[...], shift_ref[...]).astype(jnp.bfloat16)
    acc = jnp.dot(h, wbf_ref[...], preferred_element_type=jnp.float32)
    cos = cos_ref[...]
    sin = sin_ref[...]
    lane = lax.broadcasted_iota(jnp.int32, cos.shape, 1)
    first_half = (lane % HEAD_DIM) < (HEAD_DIM // 2)
    for c in range(o_ref.shape[0]):
        a = acc[:, c * LANES:(c + 1) * LANES]
        partner = jnp.where(first_half, pltpu.roll(a, LANES - HEAD_DIM // 2, 1), pltpu.roll(a, HEAD_DIM // 2, 1))
        o_ref[c] = (a * cos + partner * sin).astype(o_ref.dtype)


def _qkv_proj(x, mod6, w_in, layer, cos_t, sin_t, tm=256):
    B, S, D = x.shape
    n3 = w_in.shape[2]
    width = n3 // 3
    pairs = width // LANES
    return pl.pallas_call(
        _qkv_kernel,
        grid=(3, B, S // tm),
        in_specs=[
            pl.BlockSpec((None, tm, D), lambda j, b, i: (b, i, 0)),
            pl.BlockSpec((None, None, 1, D), lambda j, b, i: (b, 0, 0, 0)),
            pl.BlockSpec((None, None, 1, D), lambda j, b, i: (b, 1, 0, 0)),
            pl.BlockSpec((None, D, width), lambda j, b, i: (layer, 0, j), pipeline_mode=pl.Buffered(1)),
            pl.BlockSpec((None, None, tm, LANES), lambda j, b, i: (j, b, i, 0)),
            pl.BlockSpec((None, None, tm, LANES), lambda j, b, i: (j, b, i, 0)),
        ],
        out_specs=pl.BlockSpec((None, pairs, tm, LANES), lambda j, b, i: (b, j, i, 0)),
        out_shape=jax.ShapeDtypeStruct((B, 3 * pairs, S, LANES), jnp.bfloat16),
        scratch_shapes=[pltpu.VMEM((D, width), jnp.bfloat16)],
        compiler_params=_cparams(3),
        name="qkv_proj",
    )(x, mod6, mod6, w_in, cos_t, sin_t)


def _dilated_kernel(q_ref, k_ref, v_ref, b1_ref, b4_ref, b16_ref, o_ref, m_ref, l_ref, acc_ref):
    nb = q_ref.shape[0]

    def rows(x):
        return x.reshape(-1, LANES)

    def update(items):
        old = [(acc_ref[idx], m_ref[(0,) + idx], m_ref[(1,) + idx], l_ref[(0,) + idx], l_ref[(1,) + idx])
               for (_, _, _, _, idx) in items]
        new = []
        for (q, kk, vv, bias, idx), (a_raw, m0, m1, l0, l1) in zip(items, old):
            s = _scores(_stack_heads(q), kk) + bias
            m_old = jnp.concatenate([rows(m0), rows(m1)], axis=0)
            l_old = jnp.concatenate([rows(l0), rows(l1)], axis=0)
            a_old = jnp.concatenate([rows(a_raw), rows(a_raw)], axis=0)
            new.append(_softmax_step(s, vv, m_old, l_old, a_old) + (a_raw.shape, q.shape[0]))
        for (_, _, _, _, idx), (m_new, l_new, a_new, shp, n) in zip(items, new):
            m_ref[(0,) + idx] = m_new[:n].reshape(shp)
            m_ref[(1,) + idx] = m_new[n:].reshape(shp)
            l_ref[(0,) + idx] = l_new[:n].reshape(shp)
            l_ref[(1,) + idx] = l_new[n:].reshape(shp)
            acc_ref[idx] = _merge_heads(a_new).reshape(shp)

    def init(n, carry):
        m_ref[0, n] = jnp.full(m_ref.shape[2:], NEG, jnp.float32)
        m_ref[1, n] = jnp.full(m_ref.shape[2:], NEG, jnp.float32)
        l_ref[0, n] = jnp.zeros(l_ref.shape[2:], jnp.float32)
        l_ref[1, n] = jnp.zeros(l_ref.shape[2:], jnp.float32)
        acc_ref[n] = jnp.zeros(acc_ref.shape[1:], jnp.float32)
        return carry

    lax.fori_loop(0, nb, init, 0)

    def run(n_blocks, group, make_item):
        def step(t, carry):
            update([make_item(group * t + g) for g in range(group)])
            return carry

        lax.fori_loop(0, n_blocks // group, step, 0)

    def d1(n):
        prev = jnp.maximum(n - 1, 0)
        kk = jnp.concatenate([rows(k_ref[prev]), rows(k_ref[n])], axis=0)
        vv = jnp.concatenate([rows(v_ref[prev]), rows(v_ref[n])], axis=0)
        return rows(q_ref[n]), kk, vv, b1_ref[jnp.minimum(n, 1)], (n,)

    run(nb, 4, d1)

    per4 = nb // 2

    def d4(t):
        r4 = t // per4
        a = t % per4
        cur = pl.ds(2 * a, 2)
        prev = pl.ds(jnp.maximum(2 * a - 2, 0), 2)
        grp = pl.ds(4 * r4, 4)
        kk = jnp.concatenate([rows(k_ref[prev, grp]), rows(k_ref[cur, grp])], axis=0)
        vv = jnp.concatenate([rows(v_ref[prev, grp]), rows(v_ref[cur, grp])], axis=0)
        return rows(q_ref[cur, grp]), kk, vv, b4_ref[jnp.minimum(a, 1)], (cur, grp)

    run(4 * per4, 8, d4)

    per16 = nb // 8

    def d16(t):
        rr = t // per16
        b = t % per16
        cur = pl.ds(8 * b, 8)
        prev = pl.ds(jnp.maximum(8 * b - 8, 0), 8)
        kk = jnp.concatenate([rows(k_ref[prev, rr]), rows(k_ref[cur, rr])], axis=0)
        vv = jnp.concatenate([rows(v_ref[prev, rr]), rows(v_ref[cur, rr])], axis=0)
        return rows(q_ref[cur, rr]), kk, vv, b16_ref[jnp.minimum(b, 1)], (cur, rr)

    run(16 * per16, 8, d16)

    def fin(n, carry):
        lane = lax.broadcasted_iota(jnp.int32, acc_ref.shape[1:], 2)
        den = jnp.where(lane < HEAD_DIM, l_ref[0, n], l_ref[1, n])
        o_ref[n] = (acc_ref[n] / den).astype(o_ref.dtype)
        return carry

    lax.fori_loop(0, nb, fin, 0)


def _seq_view(a):
    B, P, S, L = a.shape
    return a.reshape(B, P, S // BLK, 16, 16, L)


def _dilated_attention(qkv, biases):
    B, P3, S, _ = qkv.shape
    P = P3 // 3
    pa = P // 2
    nb = S // BLK
    assert S % (16 * BAND) == 0, "sequence must be a whole number of dilation-16 band blocks"
    v6 = _seq_view(qkv)
    blk = (None, None, nb, 16, 16, LANES)
    b1, b4, b16 = biases
    out = pl.pallas_call(
        _dilated_kernel,
        grid=(B, pa),
        in_specs=[
            pl.BlockSpec(blk, lambda b, p: (b, p, 0, 0, 0, 0)),
            pl.BlockSpec(blk, lambda b, p: (b, P + p, 0, 0, 0, 0)),
            pl.BlockSpec(blk, lambda b, p: (b, 2 * P + p, 0, 0, 0, 0)),
            pl.BlockSpec(b1.shape, lambda b, p: (0, 0, 0)),
            pl.BlockSpec(b4.shape, lambda b, p: (0, 0, 0)),
            pl.BlockSpec(b16.shape, lambda b, p: (0, 0, 0)),
        ],
        out_specs=pl.BlockSpec(blk, lambda b, p: (b, p, 0, 0, 0, 0)),
        out_shape=jax.ShapeDtypeStruct((B, pa, nb, 16, 16, LANES), jnp.bfloat16),
        scratch_shapes=[
            pltpu.VMEM((2, nb, 16, 16, LANES), jnp.float32),
            pltpu.VMEM((2, nb, 16, 16, LANES), jnp.float32),
            pltpu.VMEM((nb, 16, 16, LANES), jnp.float32),
        ],
        compiler_params=_cparams(2),
        name="dilated_attn",
    )(v6, v6, v6, b1, b4, b16)
    return out.reshape(B, pa, S, LANES)


def _moba_kernel(q_ref, k_ref, v_ref, own_ref, hot_ref, o_ref, kmean_ref, qa_ref, sa_ref, sb_ref, m_ref, l_ref, acc_ref):
    n = pl.program_id(2)
    nb = k_ref.shape[0]
    nbp = -(-nb // 8) * 8

    def rows(x):
        return x.reshape(-1, LANES)

    @pl.when(n == 0)
    def _():
        kmean_ref[...] = jnp.zeros(kmean_ref.shape, jnp.float32)
        for j in range(nb):
            kmean_ref[pl.ds(j, 1), :] = jnp.mean(rows(k_ref[j]).astype(jnp.float32), axis=0, keepdims=True)

    qs = _stack_heads(rows(q_ref[...]))
    nq = qs.shape[0]

    gate = _scores(kmean_ref[...].astype(jnp.bfloat16), qs)[:nbp]
    jidx = lax.broadcasted_iota(jnp.int32, gate.shape, 0)
    g = jnp.where(jidx < n, gate, -jnp.inf)
    rank = jnp.zeros(gate.shape, jnp.int32)
    for jp in range(nb):
        other = g[jp:jp + 1, :]
        beats = (other > g) | ((other == g) & (jp < jidx))
        rank = rank + beats.astype(jnp.int32)
    sel = (rank < MOBA_TOPK) & (jidx < n)
    pen_t = jnp.where(sel, 0.0, NEG).astype(jnp.float32)
    pen_t = jnp.concatenate([pen_t, jnp.full((LANES - nbp, nq), NEG, jnp.float32)], axis=0)
    qa_ref[...] = jnp.concatenate([qs, pen_t.T.astype(qs.dtype)], axis=1)

    s = _scores(qs, rows(k_ref[n])) + own_ref[...]
    m0 = jnp.max(s, axis=1, keepdims=True)
    p = jnp.exp(s - m0)
    m_ref[...] = jnp.broadcast_to(m0, m_ref.shape)
    l_ref[...] = jnp.broadcast_to(jnp.sum(p, axis=1, keepdims=True), l_ref.shape)
    acc_ref[...] = jnp.dot(p.astype(jnp.bfloat16), rows(v_ref[n]), preferred_element_type=jnp.float32)

    def pair_scores(t):
        j0 = jnp.minimum(2 * t, nb - 2)
        j1 = j0 + 1
        kk = jnp.concatenate([jnp.concatenate([rows(k_ref[j0]), hot_ref[j0]], axis=1),
                              jnp.concatenate([rows(k_ref[j1]), hot_ref[j1]], axis=1)], axis=0)
        return _scores(qa_ref[...], kk)

    def absorb(s, t):
        j0 = 2 * t
        vv = jnp.concatenate([rows(v_ref[j0]), rows(v_ref[j0 + 1])], axis=0)
        m_new, l_new, a_new = _softmax_step(s, vv, m_ref[...], l_ref[...], acc_ref[...])
        m_ref[...] = m_new
        l_ref[...] = l_new
        acc_ref[...] = a_new

    sa_ref[...] = pair_scores(0)

    def past(u, carry):
        t = 2 * u
        sb_ref[...] = pair_scores(t + 1)
        absorb(sa_ref[...], t)
        sa_ref[...] = pair_scores(t + 2)
        absorb(sb_ref[...], t + 1)
        return carry

    n_pairs = (n + 1) // 2
    lax.fori_loop(0, (n_pairs + 1) // 2, past, 0)

    o_ref[...] = _merge_heads(acc_ref[...] / l_ref[...]).reshape(o_ref.shape).astype(o_ref.dtype)


def _block_one_hot(nb):
    hot = np.zeros((nb, BLK, LANES), np.float32)
    hot[np.arange(nb), :, np.arange(nb)] = 1.0
    return jnp.asarray(hot, jnp.bfloat16)


def _moba_attention(qkv, own_bias):
    B, P3, S, _ = qkv.shape
    P = P3 // 3
    pa = P // 2
    nb = S // BLK
    assert nb <= LANES
    v6 = _seq_view(qkv)
    hot = _block_one_hot(nb)
    full = (None, None, nb, 16, 16, LANES)
    tile = (None, None, None, 16, 16, LANES)
    out = pl.pallas_call(
        _moba_kernel,
        grid=(B, pa, nb),
        in_specs=[
            pl.BlockSpec(tile, lambda b, p, n: (b, pa + p, n, 0, 0, 0)),
            pl.BlockSpec(full, lambda b, p, n: (b, P + pa + p, 0, 0, 0, 0)),
            pl.BlockSpec(full, lambda b, p, n: (b, 2 * P + pa + p, 0, 0, 0, 0)),
            pl.BlockSpec(own_bias.shape, lambda b, p, n: (0, 0)),
            pl.BlockSpec(hot.shape, lambda b, p, n: (0, 0, 0)),
        ],
        out_specs=pl.BlockSpec(tile, lambda b, p, n: (b, p, n, 0, 0, 0)),
        out_shape=jax.ShapeDtypeStruct((B, pa, nb, 16, 16, LANES), jnp.bfloat16),
        scratch_shapes=[
            pltpu.VMEM((LANES, LANES), jnp.float32),
            pltpu.VMEM((2 * BLK, 2 * LANES), jnp.bfloat16),
            pltpu.VMEM((2 * BLK, 2 * BLK), jnp.float32),
            pltpu.VMEM((2 * BLK, 2 * BLK), jnp.float32),
            pltpu.VMEM((2 * BLK, LANES), jnp.float32),
            pltpu.VMEM((2 * BLK, LANES), jnp.float32),
            pltpu.VMEM((2 * BLK, LANES), jnp.float32),
        ],
        compiler_params=_cparams(3),
        name="moba_attn",
    )(v6, v6, v6, own_bias, hot)
    return out.reshape(B, pa, S, LANES)


def _post_kernel(oa_ref, ob_ref, x_ref, gate_a_ref, shift_f_ref, scale_f_ref, beta_ref, w_ref, wr_ref, br_ref,
                 x1_ref, h2_ref, te_ref, tg_ref, wbf_ref, y_ref):
    b = pl.program_id(0)
    i = pl.program_id(1)

    @pl.when((b == 0) & (i == 0))
    def _():
        _cast_rows(w_ref, wbf_ref)

    half = oa_ref.shape[0] * LANES
    for off, o_ref in ((0, oa_ref), (half, ob_ref)):
        ss = None
        for c in range(o_ref.shape[0]):
            o = o_ref[c].astype(jnp.float32)
            part = jnp.sum(o * o, axis=-1, keepdims=True)
            ss = part if ss is None else ss + part
        inv = lax.rsqrt(ss / half + NORM_EPS)
        for c in range(o_ref.shape[0]):
            lo = off + c * LANES
            y_ref[:, lo:lo + LANES] = (o_ref[c].astype(jnp.float32) * inv * beta_ref[:, lo:lo + LANES]).astype(y_ref.dtype)

    mixed = jnp.dot(y_ref[...], wbf_ref[...], preferred_element_type=jnp.float32)
    x1 = x_ref[...] + gate_a_ref[...] * mixed
    x1_ref[...] = x1
    h2 = _rms_mod(x1, scale_f_ref[...], shift_f_ref[...])
    h2_ref[...] = h2

    logits = jnp.dot(h2, wr_ref[...], preferred_element_type=jnp.float32, precision=lax.Precision.HIGHEST) + br_ref[...]
    lane = lax.broadcasted_iota(jnp.int32, logits.shape, 1).astype(jnp.float32)
    te = jnp.zeros(logits.shape, jnp.float32)
    tg = jnp.zeros(logits.shape, jnp.float32)
    work = logits
    top0 = None
    den = None
    for kk in range(TOP_K):
        mv = jnp.max(work, axis=1, keepdims=True)
        idx = jnp.min(jnp.where(work == mv, lane, float(LANES)), axis=1, keepdims=True)
        work = jnp.where(lane == idx, -jnp.inf, work)
        if kk == 0:
            top0 = mv
        e = jnp.exp(mv - top0)
        den = e if den is None else den + e
        te = jnp.where(lane == kk, idx, te)
        tg = jnp.where(lane == kk, e, tg)
    te_ref[...] = te.astype(jnp.int32)
    tg_ref[...] = tg / den


def _post_attention(oa, ob, x, mod6, beta, w_out, layer, w_router, b_router, tm=256):
    B, S, D = x.shape
    pa = oa.shape[1]
    ne = w_router.shape[1]
    assert TOP_K <= ne <= LANES
    w_router = jnp.pad(w_router, ((0, 0), (0, LANES - ne)))
    b_router = jnp.pad(b_router, (0, LANES - ne), constant_values=-jnp.inf)
    ne = LANES
    row = lambda k: pl.BlockSpec((None, None, 1, D), lambda b, i, k=k: (b, k, 0, 0))
    tok = pl.BlockSpec((None, tm, D), lambda b, i: (b, i, 0))
    small = pl.BlockSpec((None, tm, LANES), lambda b, i: (b, i, 0))
    return pl.pallas_call(
        _post_kernel,
        grid=(B, S // tm),
        in_specs=[
            pl.BlockSpec((None, pa, tm, LANES), lambda b, i: (b, 0, i, 0)),
            pl.BlockSpec((None, pa, tm, LANES), lambda b, i: (b, 0, i, 0)),
            tok, row(2), row(3), row(4),
            pl.BlockSpec((1, D), lambda b, i: (0, 0)),
            pl.BlockSpec((None,) + w_out.shape[1:], lambda b, i: (layer, 0, 0), pipeline_mode=pl.Buffered(1)),
            pl.BlockSpec(w_router.shape, lambda b, i: (0, 0)),
            pl.BlockSpec((1, ne), lambda b, i: (0, 0)),
        ],
        out_specs=[tok, tok, small, small],
        out_shape=[
            jax.ShapeDtypeStruct((B, S, D), jnp.float32),
            jax.ShapeDtypeStruct((B, S, D), jnp.float32),
            jax.ShapeDtypeStruct((B, S, LANES), jnp.int32),
            jax.ShapeDtypeStruct((B, S, LANES), jnp.float32),
        ],
        scratch_shapes=[pltpu.VMEM(w_out.shape[1:], jnp.bfloat16), pltpu.VMEM((tm, D), jnp.bfloat16)],
        compiler_params=_cparams(2),
        name="post_attn_router",
    )(oa, ob, x, mod6, mod6, mod6, beta, w_out, w_router, b_router.reshape(1, ne))


def _route(top_e, n_experts):
    T = top_e.shape[0]
    A = T * TOP_K
    e = top_e.reshape(A)
    onehot = (e[:, None] == jnp.arange(n_experts, dtype=jnp.int32)[None, :]).astype(jnp.int32)
    csum = jnp.cumsum(onehot, axis=0)
    rank = jnp.take_along_axis(csum, e[:, None], axis=1)[:, 0] - 1
    counts = csum[-1]
    padded = ((counts + EXPERT_TILE - 1) // EXPERT_TILE) * EXPERT_TILE
    pend = jnp.cumsum(padded)
    pstart = pend - padded
    dest = (pstart[e] + rank).astype(jnp.int32)
    n_slots = A + n_experts * EXPERT_TILE
    n_blocks = n_slots // EXPERT_TILE
    n_used = (pend[-1] // EXPERT_TILE).astype(jnp.int32)
    blk = jnp.arange(n_blocks, dtype=jnp.int32)
    be = jnp.sum((pend[None, :] <= (blk * EXPERT_TILE)[:, None]).astype(jnp.int32), axis=1)
    be = jnp.minimum(be, n_experts - 1)
    be = jnp.where(blk < n_used, be, be[jnp.maximum(n_used - 1, 0)])
    return dest.reshape(T, TOP_K), pend.astype(jnp.int32), be, n_used.reshape(1)


def _tile_major(dest, tm):
    T = dest.shape[0]
    return dest.reshape(T // tm, tm, TOP_K).transpose(0, 2, 1).reshape(T // tm, 1, TOP_K * tm)


def _issue_pairs(n, make_copy, wait=False):
    def body(u, c):
        for par in range(2):
            cp = make_copy(2 * u + par)
            if wait:
                cp.wait()
            else:
                cp.start(priority=par)
        return c

    lax.fori_loop(0, n // 2, body, 0, unroll=4)


def _dispatch_kernel(pend_ref, dest_ref, h_ref, xs_hbm, zbuf, sem):
    i = pl.program_id(0)
    tm = dest_ref.shape[1] // TOP_K
    tile = zbuf.shape[0]
    ne = pend_ref.shape[0]

    @pl.when(i == 0)
    def _():
        zbuf[...] = jnp.zeros(zbuf.shape, zbuf.dtype)

        def fill(e):
            end = pend_ref[e]
            start = jnp.where(e == 0, 0, pend_ref[jnp.maximum(e - 1, 0)])
            last = pl.multiple_of(jnp.maximum(end - tile, 0), tile)
            return end > start, pltpu.make_async_copy(zbuf, xs_hbm.at[pl.ds(last, tile)], sem)

        def start(e, c):
            ok, cp = fill(e)

            @pl.when(ok)
            def _():
                cp.start()
            return c

        def wait(e, c):
            ok, cp = fill(e)

            @pl.when(ok)
            def _():
                cp.wait()
            return c

        lax.fori_loop(0, ne, start, 0)
        lax.fori_loop(0, ne, wait, 0)

        def tail(b):
            off = pl.multiple_of(b * tile, tile)
            return off >= pend_ref[ne - 1], pltpu.make_async_copy(zbuf, xs_hbm.at[pl.ds(off, tile)], sem)

        def tail_start(b, c):
            ok, cp = tail(b)

            @pl.when(ok)
            def _():
                cp.start()
            return c

        def tail_wait(b, c):
            ok, cp = tail(b)

            @pl.when(ok)
            def _():
                cp.wait()
            return c

        n_tiles = xs_hbm.shape[0] // tile
        lax.fori_loop(0, n_tiles, tail_start, 0)
        lax.fori_loop(0, n_tiles, tail_wait, 0)

    def row_copy(r):
        return pltpu.make_async_copy(h_ref.at[pl.ds(r % tm, 1)], xs_hbm.at[pl.ds(dest_ref[0, r], 1)], sem)

    _issue_pairs(TOP_K * tm, row_copy)
    _issue_pairs(TOP_K * tm, row_copy, wait=True)


def _dispatch(h2, dest, pend, n_slots, tm=128):
    T, D = h2.shape
    grid_spec = pltpu.PrefetchScalarGridSpec(
        num_scalar_prefetch=1,
        grid=(T // tm,),
        in_specs=[
            pl.BlockSpec((None, 1, TOP_K * tm), lambda i, pe: (i, 0, 0), memory_space=pltpu.SMEM),
            pl.BlockSpec((tm, D), lambda i, pe: (i, 0)),
        ],
        out_specs=pl.BlockSpec(memory_space=pl.ANY),
        scratch_shapes=[pltpu.VMEM((EXPERT_TILE, D), h2.dtype), pltpu.SemaphoreType.DMA(())],
    )
    return pl.pallas_call(
        _dispatch_kernel,
        grid_spec=grid_spec,
        out_shape=jax.ShapeDtypeStruct((n_slots, D), h2.dtype),
        compiler_params=_cparams(1),
        name="moe_dispatch",
    )(pend, _tile_major(dest, tm), h2)


def _expert_kernel(be_ref, nused_ref, x_ref, wu_ref, bu_ref, wd_ref, bd_ref, o_ref, wub, wdb):
    i = pl.program_id(0)
    d_ff = wd_ref.shape[0]

    @pl.when(i < nused_ref[0])
    def _():
        @pl.when((i == 0) | (be_ref[i] != be_ref[jnp.maximum(i - 1, 0)]))
        def _():
            _cast_rows(wu_ref, wub)
            _cast_rows(wd_ref, wdb)

        x = x_ref[...].astype(jnp.bfloat16)
        u = jnp.dot(x, wub[...], preferred_element_type=jnp.float32) + bu_ref[...]
        glu = jnp.minimum(u[:, :d_ff], SWIGLU_LIMIT)
        lin = jnp.clip(u[:, d_ff:], -SWIGLU_LIMIT, SWIGLU_LIMIT)
        act = glu * (1.0 / (1.0 + jnp.exp(-SWIGLU_ALPHA * glu))) * (lin + 1.0)
        o_ref[...] = jnp.dot(act.astype(jnp.bfloat16), wdb[...], preferred_element_type=jnp.float32) + bd_ref[...]

    @pl.when(i >= nused_ref[0])
    def _():
        o_ref[...] = jnp.zeros(o_ref.shape, o_ref.dtype)


def _experts(xs, be, n_used, layer, w_up, b_up, w_down, b_down):
    n_slots, D = xs.shape
    depth, ne, _, f2 = w_up.shape
    d_ff = w_down.shape[2]
    tm = EXPERT_TILE
    n_blocks = n_slots // tm
    grid_spec = pltpu.PrefetchScalarGridSpec(
        num_scalar_prefetch=2,
        grid=(n_blocks,),
        in_specs=[
            pl.BlockSpec((tm, D), lambda i, be, nu: (jnp.minimum(i, nu[0] - 1), 0)),
            pl.BlockSpec((None, None, D, f2), lambda i, be, nu: (layer, be[i], 0, 0)),
            pl.BlockSpec((None, None, 1, f2), lambda i, be, nu: (layer, be[i], 0, 0)),
            pl.BlockSpec((None, None, d_ff, D), lambda i, be, nu: (layer, be[i], 0, 0)),
            pl.BlockSpec((None, None, 1, D), lambda i, be, nu: (layer, be[i], 0, 0)),
        ],
        out_specs=pl.BlockSpec((tm, D), lambda i, be, nu: (i, 0)),
        scratch_shapes=[
            pltpu.VMEM((D, f2), jnp.bfloat16),
            pltpu.VMEM((d_ff, D), jnp.bfloat16),
        ],
    )
    return pl.pallas_call(
        _expert_kernel,
        grid_spec=grid_spec,
        out_shape=jax.ShapeDtypeStruct((n_slots, D), jnp.float32),
        compiler_params=_cparams(1),
        name="experts",
    )(be, n_used, xs, w_up, b_up.reshape(depth, ne, 1, f2), w_down, b_down.reshape(depth, ne, 1, D))


def _combine_kernel(dest_ref, y_hbm, x_ref, gate_ref, tg_ref, o_ref, ybuf, sem):
    tm = x_ref.shape[0]

    def row_copy(r):
        return pltpu.make_async_copy(y_hbm.at[pl.ds(dest_ref[0, r], 1)], ybuf.at[pl.ds(r, 1)], sem)

    _issue_pairs(TOP_K * tm, row_copy)
    _issue_pairs(TOP_K * tm, row_copy, wait=True)
    tg = tg_ref[...]
    y = None
    for k in range(TOP_K):
        part = tg[:, k:k + 1] * ybuf[k * tm:(k + 1) * tm, :]
        y = part if y is None else y + part
    o_ref[...] = x_ref[...] + gate_ref[...] * y


def _combine(outs, dest, tg, x1, mod6, tm=128):
    B, S, D = x1.shape
    nt = S // tm
    tok = pl.BlockSpec((None, tm, D), lambda b, i: (b, i, 0))
    return pl.pallas_call(
        _combine_kernel,
        grid=(B, nt),
        in_specs=[
            pl.BlockSpec((None, 1, TOP_K * tm), lambda b, i: (b * nt + i, 0, 0), memory_space=pltpu.SMEM),
            pl.BlockSpec(memory_space=pl.ANY),
            tok,
            pl.BlockSpec((None, None, 1, D), lambda b, i: (b, 5, 0, 0)),
            pl.BlockSpec((None, tm, LANES), lambda b, i: (b, i, 0)),
        ],
        out_specs=tok,
        out_shape=jax.ShapeDtypeStruct((B, S, D), jnp.float32),
        scratch_shapes=[pltpu.VMEM((TOP_K * tm, D), jnp.float32), pltpu.SemaphoreType.DMA(())],
        compiler_params=_cparams(2),
        name="moe_combine",
    )(_tile_major(dest, tm), outs, x1, mod6, tg)


def _final_kernel(x_ref, g_ref, o_ref):
    x = x_ref[...]
    ms = jnp.mean(x * x, axis=-1, keepdims=True)
    o_ref[...] = x * lax.rsqrt(ms + NORM_EPS) * g_ref[...]


def _final_norm(x, g_final, tm=512):
    B, S, D = x.shape
    tok = pl.BlockSpec((None, tm, D), lambda b, i: (b, i, 0))
    return pl.pallas_call(
        _final_kernel,
        grid=(B, S // tm),
        in_specs=[tok, pl.BlockSpec((1, D), lambda b, i: (0, 0))],
        out_specs=tok,
        out_shape=jax.ShapeDtypeStruct((B, S, D), jnp.float32),
        compiler_params=_cparams(2),
        name="final_norm",
    )(x, g_final.reshape(1, D))


def _rotary_tables(positions, scale_q):
    inv_freq = 1.0 / (ROPE_THETA ** (jnp.arange(0, HEAD_DIM, 2, dtype=jnp.float32) / HEAD_DIM))
    ang = positions[..., None].astype(jnp.float32) * inv_freq
    cos, sin = jnp.cos(ang), jnp.sin(ang)
    reps = LANES // HEAD_DIM
    cos_l = jnp.tile(jnp.concatenate([cos, cos], axis=-1), (1, 1, reps))
    sin_l = jnp.tile(jnp.concatenate([-sin, sin], axis=-1), (1, 1, reps))
    cos_t = jnp.stack([cos_l * scale_q, cos_l, jnp.ones_like(cos_l)])
    sin_t = jnp.stack([sin_l * scale_q, sin_l, jnp.zeros_like(sin_l)])
    return cos_t, sin_t


def kernel(x, c, positions, w_ada, b_ada, w_in, beta_a, beta_b, w_out, w_router, b_router, w_up, b_up, w_down, b_down, g_final):
    B, S, D = x.shape
    depth = w_ada.shape[0]
    ne = w_router.shape[-1]
    biases = tuple(jnp.asarray(b) for b in _dilated_biases())
    own_bias = jnp.asarray(_own_bias())

    xp = _to_perm(x)
    pos_p = _to_perm(positions)
    cos_t, sin_t = _rotary_tables(pos_p, HEAD_DIM ** -0.5)
    mod = _ada_mod(c, w_ada, b_ada)

    for l in range(depth):
        mod6 = mod[l, :B].reshape(B, 6, 1, D)
        qkv = _qkv_proj(xp, mod6, w_in, l, cos_t, sin_t)
        oa = _dilated_attention(qkv, biases)
        ob = _moba_attention(qkv, own_bias)
        beta = jnp.concatenate([beta_a[l], beta_b[l]]).reshape(1, D)
        x1, h2, te, tg = _post_attention(oa, ob, xp, mod6, beta, w_out, l, w_router[l], b_router[l])
        dest, pend, be, n_used = _route(te.reshape(B * S, LANES)[:, :TOP_K], ne)
        xs = _dispatch(h2.reshape(B * S, D), dest, pend, be.shape[0] * EXPERT_TILE)
        outs = _experts(xs, be, n_used, l, w_up, b_up, w_down, b_down)
        xp = _combine(outs, dest, tg, x1, mod6)

    return _from_perm(_final_norm(xp, g_final))
```

```python
import functools
import math

import numpy as np
import jax
import jax.numpy as jnp
from jax import lax
from jax.experimental import pallas as pl
from jax.experimental.pallas import tpu as pltpu

HEAD_DIM = 64
LANES = 128
BLK = 256
BAND = 128
DILATIONS = (1, 4, 16)
MOBA_TOPK = 3
TOP_K = 4
SWIGLU_LIMIT = 7.0
SWIGLU_ALPHA = 1.702
ROPE_THETA = 10000.0
NORM_EPS = 1e-6
NEG = -1e30
EXPERT_TILE = 256
VMEM_LIMIT = 60000 * 1024

_ARB = pltpu.ARBITRARY


def _cparams(n_axes):
    return pltpu.CompilerParams(dimension_semantics=(_ARB,) * n_axes, vmem_limit_bytes=VMEM_LIMIT)


def _to_perm(x):
    B, S = x.shape[:2]
    rest = x.shape[2:]
    x = x.reshape(B, S // BLK, 16, 4, 4, *rest)
    x = jnp.swapaxes(x, 2, 4)
    return x.reshape(B, S, *rest)


def _from_perm(x):
    B, S = x.shape[:2]
    rest = x.shape[2:]
    x = x.reshape(B, S // BLK, 4, 4, 16, *rest)
    x = jnp.swapaxes(x, 2, 4)
    return x.reshape(B, S, *rest)


def _local_pos():
    p = np.arange(BLK)
    return 16 * (p % 16) + 4 * ((p // 16) % 4) + p // 64


def _band_bias(seq_q, seq_k):
    dist = seq_q[:, None] - seq_k[None, :]
    ok = (dist >= 0) & (dist <= BAND)
    half = seq_k.shape[0] // 2
    exists = np.arange(seq_k.shape[0])[None, :] >= half
    normal = np.where(ok, 0.0, NEG).astype(np.float32)
    first = np.where(ok & exists, 0.0, NEG).astype(np.float32)
    two = np.stack([first, normal])
    return np.concatenate([two, two], axis=1)


def _dilated_biases():
    lp = _local_pos()
    b1 = _band_bias(lp + BLK, np.concatenate([lp, lp + BLK]))
    hi, c, lo = np.meshgrid(np.arange(4), np.arange(4), np.arange(16), indexing="ij")
    seq_k4 = (64 * hi + 4 * lo + c).reshape(-1)
    seq_q4 = seq_k4[:128] + 128
    b4 = _band_bias(seq_q4, seq_k4)
    b16 = _band_bias(np.arange(128) + 128, np.arange(256))
    return b1, b4, b16


def _own_bias():
    lp = _local_pos()
    ok = lp[None, :] <= lp[:, None]
    m = np.where(ok, 0.0, NEG).astype(np.float32)
    return np.concatenate([m, m], axis=0)


def _stack_heads(q):
    lane = lax.broadcasted_iota(jnp.int32, q.shape, 1)
    zero = jnp.zeros_like(q)
    return jnp.concatenate([jnp.where(lane < HEAD_DIM, q, zero), jnp.where(lane >= HEAD_DIM, q, zero)], axis=0)


def _scores(qs, kk):
    return lax.dot_general(qs, kk, (((1,), (1,)), ((), ())), preferred_element_type=jnp.float32)


def _tile_lanes(x, k):
    return x if k == 1 else jnp.concatenate([x] * k, axis=1)


def _softmax_step(s, vv, m_old, l_old, acc_old):
    m_cur = jnp.max(s, axis=1, keepdims=True)
    m_new = jnp.maximum(m_old, m_cur)
    alpha = jnp.exp(m_old - m_new)
    p = jnp.exp(s - _tile_lanes(m_new, s.shape[1] // LANES))
    l_new = alpha * l_old + jnp.sum(p, axis=1, keepdims=True)
    pv = jnp.dot(p.astype(vv.dtype), vv, preferred_element_type=jnp.float32)
    return m_new, l_new, alpha * acc_old + pv


def _merge_heads(x):
    n = x.shape[0] // 2
    lane = lax.broadcasted_iota(jnp.int32, (n, LANES), 1)
    return jnp.where(lane < HEAD_DIM, x[:n], x[n:])


def _ada_kernel(c_ref, w_ref, b_ref, o_ref):
    c = c_ref[...]
    sc = (c * (1.0 / (1.0 + jnp.exp(-c)))).astype(jnp.bfloat16)
    o_ref[...] = jnp.dot(sc, w_ref[...].astype(jnp.bfloat16), preferred_element_type=jnp.float32) + b_ref[...]


def _ada_mod(c, w_ada, b_ada):
    depth, d, n = w_ada.shape
    rows = 8
    tn = math.gcd(n, 1024)
    cp = jnp.zeros((rows, d), jnp.float32).at[: c.shape[0]].set(c)
    return pl.pallas_call(
        _ada_kernel,
        grid=(depth, n // tn),
        in_specs=[
            pl.BlockSpec((rows, d), lambda l, j: (0, 0)),
            pl.BlockSpec((None, d, tn), lambda l, j: (l, 0, j)),
            pl.BlockSpec((None, 1, tn), lambda l, j: (l, 0, j)),
        ],
        out_specs=pl.BlockSpec((None, rows, tn), lambda l, j: (l, 0, j)),
        out_shape=jax.ShapeDtypeStruct((depth, rows, n), jnp.float32),
        compiler_params=_cparams(2),
        name="ada_mod",
    )(cp, w_ada, b_ada.reshape(depth, 1, n))


def _rms_mod(x, scale, shift):
    ms = jnp.mean(x * x, axis=-1, keepdims=True)
    return x * lax.rsqrt(ms + NORM_EPS) * (1.0 + scale) + shift


def _cast_rows(src_ref, dst_ref):
    rows_per_step = math.gcd(src_ref.shape[0], 256)
    n = src_ref.shape[0] // rows_per_step

    def body(i, carry):
        r = pl.multiple_of(i * rows_per_step, rows_per_step)
        dst_ref[pl.ds(r, rows_per_step), :] = src_ref[pl.ds(r, rows_per_step), :].astype(dst_ref.dtype)
        return carry

    lax.fori_loop(0, n, body, 0)


def _qkv_kernel(x_ref, shift_ref, scale_ref, w_ref, cos_ref, sin_ref, o_ref, wbf_ref):
    b = pl.program_id(1)
    i = pl.program_id(2)

    @pl.when((b == 0) & (i == 0))
    def _():
        _cast_rows(w_ref, wbf_ref)

    h = _rms_mod(x_ref[...], scale_ref[...], shift_ref[...]).astype(jnp.bfloat16)
    acc = jnp.dot(h, wbf_ref[...], preferred_element_type=jnp.float32)
    cos = cos_ref[...]
    sin = sin_ref[...]
    lane = lax.broadcasted_iota(jnp.int32, cos.shape, 1)
    first_half = (lane % HEAD_DIM) < (HEAD_DIM // 2)
    for c in range(o_ref.shape[0]):
        a = acc[:, c * LANES:(c + 1) * LANES]
        partner = jnp.where(first_half, pltpu.roll(a, LANES - HEAD_DIM // 2, 1), pltpu.roll(a, HEAD_DIM // 2, 1))
        o_ref[c] = (a * cos + partner * sin).astype(o_ref.dtype)


def _qkv_proj(x, mod6, w_in, layer, cos_t, sin_t, tm=256):
    B, S, D = x.shape
    n3 = w_in.shape[2]
    width = n3 // 3
    pairs = width // LANES
    return pl.pallas_call(
        _qkv_kernel,
        grid=(3, B, S // tm),
        in_specs=[
            pl.BlockSpec((None, tm, D), lambda j, b, i: (b, i, 0)),
            pl.BlockSpec((None, None, 1, D), lambda j, b, i: (b, 0, 0, 0)),
            pl.BlockSpec((None, None, 1, D), lambda j, b, i: (b, 1, 0, 0)),
            pl.BlockSpec((None, D, width), lambda j, b, i: (layer, 0, j), pipeline_mode=pl.Buffered(1)),
            pl.BlockSpec((None, None, tm, LANES), lambda j, b, i: (j, b, i, 0)),
            pl.BlockSpec((None, None, tm, LANES), lambda j, b, i: (j, b, i, 0)),
        ],
        out_specs=pl.BlockSpec((None, pairs, tm, LANES), lambda j, b, i: (b, j, i, 0)),
        out_shape=jax.ShapeDtypeStruct((B, 3 * pairs, S, LANES), jnp.bfloat16),
        scratch_shapes=[pltpu.VMEM((D, width), jnp.bfloat16)],
        compiler_params=_cparams(3),
        name="qkv_proj",
    )(x, mod6, mod6, w_in, cos_t, sin_t)


def _dilated_kernel(q_ref, k_ref, v_ref, b1_ref, b4_ref, b16_ref, o_ref, m_ref, l_ref, acc_ref):
    nb = q_ref.shape[0]

    def rows(x):
        return x.reshape(-1, LANES)

    def update(items):
        old = [(acc_ref[idx], m_ref[(0,) + idx], m_ref[(1,) + idx], l_ref[(0,) + idx], l_ref[(1,) + idx])
               for (_, _, _, _, idx) in items]
        new = []
        for (q, kk, vv, bias, idx), (a_raw, m0, m1, l0, l1) in zip(items, old):
            s = _scores(_stack_heads(q), kk) + bias
            m_old = jnp.concatenate([rows(m0), rows(m1)], axis=0)
            l_old = jnp.concatenate([rows(l0), rows(l1)], axis=0)
            a_old = jnp.concatenate([rows(a_raw), rows(a_raw)], axis=0)
            new.append(_softmax_step(s, vv, m_old, l_old, a_old) + (a_raw.shape, q.shape[0]))
        for (_, _, _, _, idx), (m_new, l_new, a_new, shp, n) in zip(items, new):
            m_ref[(0,) + idx] = m_new[:n].reshape(shp)
            m_ref[(1,) + idx] = m_new[n:].reshape(shp)
            l_ref[(0,) + idx] = l_new[:n].reshape(shp)
            l_ref[(1,) + idx] = l_new[n:].reshape(shp)
            acc_ref[idx] = _merge_heads(a_new).reshape(shp)

    def init(n, carry):
        m_ref[0, n] = jnp.full(m_ref.shape[2:], NEG, jnp.float32)
        m_ref[1, n] = jnp.full(m_ref.shape[2:], NEG, jnp.float32)
        l_ref[0, n] = jnp.zeros(l_ref.shape[2:], jnp.float32)
        l_ref[1, n] = jnp.zeros(l_ref.shape[2:], jnp.float32)
        acc_ref[n] = jnp.zeros(acc_ref.shape[1:], jnp.float32)
        return carry

    lax.fori_loop(0, nb, init, 0)

    def run(n_blocks, group, make_item):
        def step(t, carry):
            update([make_item(group * t + g) for g in range(group)])
            return carry

        lax.fori_loop(0, n_blocks // group, step, 0)

    def d1(n):
        prev = jnp.maximum(n - 1, 0)
        kk = jnp.concatenate([rows(k_ref[prev]), rows(k_ref[n])], axis=0)
        vv = jnp.concatenate([rows(v_ref[prev]), rows(v_ref[n])], axis=0)
        return rows(q_ref[n]), kk, vv, b1_ref[jnp.minimum(n, 1)], (n,)

    run(nb, 4, d1)

    per4 = nb // 2

    def d4(t):
        r4 = t // per4
        a = t % per4
        cur = pl.ds(2 * a, 2)
        prev = pl.ds(jnp.maximum(2 * a - 2, 0), 2)
        grp = pl.ds(4 * r4, 4)
        kk = jnp.concatenate([rows(k_ref[prev, grp]), rows(k_ref[cur, grp])], axis=0)
        vv = jnp.concatenate([rows(v_ref[prev, grp]), rows(v_ref[cur, grp])], axis=0)
        return rows(q_ref[cur, grp]), kk, vv, b4_ref[jnp.minimum(a, 1)], (cur, grp)

    run(4 * per4, 8, d4)

    per16 = nb // 8

    def d16(t):
        rr = t // per16
        b = t % per16
        cur = pl.ds(8 * b, 8)
        prev = pl.ds(jnp.maximum(8 * b - 8, 0), 8)
        kk = jnp.concatenate([rows(k_ref[prev, rr]), rows(k_ref[cur, rr])], axis=0)
        vv = jnp.concatenate([rows(v_ref[prev, rr]), rows(v_ref[cur, rr])], axis=0)
        return rows(q_ref[cur, rr]), kk, vv, b16_ref[jnp.minimum(b, 1)], (cur, rr)

    run(16 * per16, 8, d16)

    def fin(n, carry):
        lane = lax.broadcasted_iota(jnp.int32, acc_ref.shape[1:], 2)
        den = jnp.where(lane < HEAD_DIM, l_ref[0, n], l_ref[1, n])
        o_ref[n] = (acc_ref[n] / den).astype(o_ref.dtype)
        return carry

    lax.fori_loop(0, nb, fin, 0)


def _seq_view(a):
    B, P, S, L = a.shape
    return a.reshape(B, P, S // BLK, 16, 16, L)


def _dilated_attention(qkv, biases):
    B, P3, S, _ = qkv.shape
    P = P3 // 3
    pa = P // 2
    nb = S // BLK
    assert S % (16 * BAND) == 0, "sequence must be a whole number of dilation-16 band blocks"
    v6 = _seq_view(qkv)
    blk = (None, None, nb, 16, 16, LANES)
    b1, b4, b16 = biases
    out = pl.pallas_call(
        _dilated_kernel,
        grid=(B, pa),
        in_specs=[
            pl.BlockSpec(blk, lambda b, p: (b, p, 0, 0, 0, 0)),
            pl.BlockSpec(blk, lambda b, p: (b, P + p, 0, 0, 0, 0)),
            pl.BlockSpec(blk, lambda b, p: (b, 2 * P + p, 0, 0, 0, 0)),
            pl.BlockSpec(b1.shape, lambda b, p: (0, 0, 0)),
            pl.BlockSpec(b4.shape, lambda b, p: (0, 0, 0)),
            pl.BlockSpec(b16.shape, lambda b, p: (0, 0, 0)),
        ],
        out_specs=pl.BlockSpec(blk, lambda b, p: (b, p, 0, 0, 0, 0)),
        out_shape=jax.ShapeDtypeStruct((B, pa, nb, 16, 16, LANES), jnp.bfloat16),
        scratch_shapes=[
            pltpu.VMEM((2, nb, 16, 16, LANES), jnp.float32),
            pltpu.VMEM((2, nb, 16, 16, LANES), jnp.float32),
            pltpu.VMEM((nb, 16, 16, LANES), jnp.float32),
        ],
        compiler_params=_cparams(2),
        name="dilated_attn",
    )(v6, v6, v6, b1, b4, b16)
    return out.reshape(B, pa, S, LANES)


def _moba_kernel(q_ref, k_ref, v_ref, own_ref, hot_ref, o_ref, kmean_ref, qa_ref, sa_ref, sb_ref, m_ref, l_ref, acc_ref):
    n = pl.program_id(2)
    nb = k_ref.shape[0]
    nbp = -(-nb // 8) * 8

    def rows(x):
        return x.reshape(-1, LANES)

    @pl.when(n == 0)
    def _():
        kmean_ref[...] = jnp.zeros(kmean_ref.shape, jnp.float32)
        for j in range(nb):
            kmean_ref[pl.ds(j, 1), :] = jnp.mean(rows(k_ref[j]).astype(jnp.float32), axis=0, keepdims=True)

    qs = _stack_heads(rows(q_ref[...]))
    nq = qs.shape[0]

    gate = _scores(kmean_ref[...].astype(jnp.bfloat16), qs)[:nbp]
    jidx = lax.broadcasted_iota(jnp.int32, gate.shape, 0)
    g = jnp.where(jidx < n, gate, -jnp.inf)
    rank = jnp.zeros(gate.shape, jnp.int32)
    for jp in range(nb):
        other = g[jp:jp + 1, :]
        beats = (other > g) | ((other == g) & (jp < jidx))
        rank = rank + beats.astype(jnp.int32)
    sel = (rank < MOBA_TOPK) & (jidx < n)
    pen_t = jnp.where(sel, 0.0, NEG).astype(jnp.float32)
    pen_t = jnp.concatenate([pen_t, jnp.full((LANES - nbp, nq), NEG, jnp.float32)], axis=0)
    qa_ref[...] = jnp.concatenate([qs, pen_t.T.astype(qs.dtype)], axis=1)

    s = _scores(qs, rows(k_ref[n])) + own_ref[...]
    m0 = jnp.max(s, axis=1, keepdims=True)
    p = jnp.exp(s - m0)
    m_ref[...] = jnp.broadcast_to(m0, m_ref.shape)
    l_ref[...] = jnp.broadcast_to(jnp.sum(p, axis=1, keepdims=True), l_ref.shape)
    acc_ref[...] = jnp.dot(p.astype(jnp.bfloat16), rows(v_ref[n]), preferred_element_type=jnp.float32)

    def pair_scores(t):
        j0 = jnp.minimum(2 * t, nb - 2)
        j1 = j0 + 1
        kk = jnp.concatenate([jnp.concatenate([rows(k_ref[j0]), hot_ref[j0]], axis=1),
                              jnp.concatenate([rows(k_ref[j1]), hot_ref[j1]], axis=1)], axis=0)
        return _scores(qa_ref[...], kk)

    def absorb(s, t):
        j0 = 2 * t
        vv = jnp.concatenate([rows(v_ref[j0]), rows(v_ref[j0 + 1])], axis=0)
        m_new, l_new, a_new = _softmax_step(s, vv, m_ref[...], l_ref[...], acc_ref[...])
        m_ref[...] = m_new
        l_ref[...] = l_new
        acc_ref[...] = a_new

    sa_ref[...] = pair_scores(0)

    def past(u, carry):
        t = 2 * u
        sb_ref[...] = pair_scores(t + 1)
        absorb(sa_ref[...], t)
        sa_ref[...] = pair_scores(t + 2)
        absorb(sb_ref[...], t + 1)
        return carry

    n_pairs = (n + 1) // 2
    lax.fori_loop(0, (n_pairs + 1) // 2, past, 0)

    o_ref[...] = _merge_heads(acc_ref[...] / l_ref[...]).reshape(o_ref.shape).astype(o_ref.dtype)


def _block_one_hot(nb):
    hot = np.zeros((nb, BLK, LANES), np.float32)
    hot[np.arange(nb), :, np.arange(nb)] = 1.0
    return jnp.asarray(hot, jnp.bfloat16)


def _moba_attention(qkv, own_bias):
    B, P3, S, _ = qkv.shape
    P = P3 // 3
    pa = P // 2
    nb = S // BLK
    assert nb <= LANES
    v6 = _seq_view(qkv)
    hot = _block_one_hot(nb)
    full = (None, None, nb, 16, 16, LANES)
    tile = (None, None, None, 16, 16, LANES)
    out = pl.pallas_call(
        _moba_kernel,
        grid=(B, pa, nb),
        in_specs=[
            pl.BlockSpec(tile, lambda b, p, n: (b, pa + p, n, 0, 0, 0)),
            pl.BlockSpec(full, lambda b, p, n: (b, P + pa + p, 0, 0, 0, 0)),
            pl.BlockSpec(full, lambda b, p, n: (b, 2 * P + pa + p, 0, 0, 0, 0)),
            pl.BlockSpec(own_bias.shape, lambda b, p, n: (0, 0)),
            pl.BlockSpec(hot.shape, lambda b, p, n: (0, 0, 0)),
        ],
        out_specs=pl.BlockSpec(tile, lambda b, p, n: (b, p, n, 0, 0, 0)),
        out_shape=jax.ShapeDtypeStruct((B, pa, nb, 16, 16, LANES), jnp.bfloat16),
        scratch_shapes=[
            pltpu.VMEM((LANES, LANES), jnp.float32),
            pltpu.VMEM((2 * BLK, 2 * LANES), jnp.bfloat16),
            pltpu.VMEM((2 * BLK, 2 * BLK), jnp.float32),
            pltpu.VMEM((2 * BLK, 2 * BLK), jnp.float32),
            pltpu.VMEM((2 * BLK, LANES), jnp.float32),
            pltpu.VMEM((2 * BLK, LANES), jnp.float32),
            pltpu.VMEM((2 * BLK, LANES), jnp.float32),
        ],
        compiler_params=_cparams(3),
        name="moba_attn",
    )(v6, v6, v6, own_bias, hot)
    return out.reshape(B, pa, S, LANES)


def _post_kernel(oa_ref, ob_ref, x_ref, gate_a_ref, shift_f_ref, scale_f_ref, beta_ref, w_ref, wr_ref, br_ref,
                 x1_ref, h2_ref, te_ref, tg_ref, wbf_ref, y_ref):
    b = pl.program_id(0)
    i = pl.program_id(1)

    @pl.when((b == 0) & (i == 0))
    def _():
        _cast_rows(w_ref, wbf_ref)

    half = oa_ref.shape[0] * LANES
    for off, o_ref in ((0, oa_ref), (half, ob_ref)):
        ss = None
        for c in range(o_ref.shape[0]):
            o = o_ref[c].astype(jnp.float32)
            part = jnp.sum(o * o, axis=-1, keepdims=True)
            ss = part if ss is None else ss + part
        inv = lax.rsqrt(ss / half + NORM_EPS)
        for c in range(o_ref.shape[0]):
            lo = off + c * LANES
            y_ref[:, lo:lo + LANES] = (o_ref[c].astype(jnp.float32) * inv * beta_ref[:, lo:lo + LANES]).astype(y_ref.dtype)

    mixed = jnp.dot(y_ref[...], wbf_ref[...], preferred_element_type=jnp.float32)
    x1 = x_ref[...] + gate_a_ref[...] * mixed
    x1_ref[...] = x1
    h2 = _rms_mod(x1, scale_f_ref[...], shift_f_ref[...])
    h2_ref[...] = h2

    logits = jnp.dot(h2, wr_ref[...], preferred_element_type=jnp.float32, precision=lax.Precision.HIGHEST) + br_ref[...]
    lane = lax.broadcasted_iota(jnp.int32, logits.shape, 1).astype(jnp.float32)
    te = jnp.zeros(logits.shape, jnp.float32)
    tg = jnp.zeros(logits.shape, jnp.float32)
    work = logits
    top0 = None
    den = None
    for kk in range(TOP_K):
        mv = jnp.max(work, axis=1, keepdims=True)
        idx = jnp.min(jnp.where(work == mv, lane, float(LANES)), axis=1, keepdims=True)
        work = jnp.where(lane == idx, -jnp.inf, work)
        if kk == 0:
            top0 = mv
        e = jnp.exp(mv - top0)
        den = e if den is None else den + e
        te = jnp.where(lane == kk, idx, te)
        tg = jnp.where(lane == kk, e, tg)
    te_ref[...] = te.astype(jnp.int32)
    tg_ref[...] = tg / den


def _post_attention(oa, ob, x, mod6, beta, w_out, layer, w_router, b_router, tm=256):
    B, S, D = x.shape
    pa = oa.shape[1]
    ne = w_router.shape[1]
    assert TOP_K <= ne <= LANES
    w_router = jnp.pad(w_router, ((0, 0), (0, LANES - ne)))
    b_router = jnp.pad(b_router, (0, LANES - ne), constant_values=-jnp.inf)
    ne = LANES
    row = lambda k: pl.BlockSpec((None, None, 1, D), lambda b, i, k=k: (b, k, 0, 0))
    tok = pl.BlockSpec((None, tm, D), lambda b, i: (b, i, 0))
    small = pl.BlockSpec((None, tm, LANES), lambda b, i: (b, i, 0))
    return pl.pallas_call(
        _post_kernel,
        grid=(B, S // tm),
        in_specs=[
            pl.BlockSpec((None, pa, tm, LANES), lambda b, i: (b, 0, i, 0)),
            pl.BlockSpec((None, pa, tm, LANES), lambda b, i: (b, 0, i, 0)),
            tok, row(2), row(3), row(4),
            pl.BlockSpec((1, D), lambda b, i: (0, 0)),
            pl.BlockSpec((None,) + w_out.shape[1:], lambda b, i: (layer, 0, 0), pipeline_mode=pl.Buffered(1)),
            pl.BlockSpec(w_router.shape, lambda b, i: (0, 0)),
            pl.BlockSpec((1, ne), lambda b, i: (0, 0)),
        ],
        out_specs=[tok, tok, small, small],
        out_shape=[
            jax.ShapeDtypeStruct((B, S, D), jnp.float32),
            jax.ShapeDtypeStruct((B, S, D), jnp.float32),
            jax.ShapeDtypeStruct((B, S, LANES), jnp.int32),
            jax.ShapeDtypeStruct((B, S, LANES), jnp.float32),
        ],
        scratch_shapes=[pltpu.VMEM(w_out.shape[1:], jnp.bfloat16), pltpu.VMEM((tm, D), jnp.bfloat16)],
        compiler_params=_cparams(2),
        name="post_attn_router",
    )(oa, ob, x, mod6, mod6, mod6, beta, w_out, w_router, b_router.reshape(1, ne))


def _route(top_e, n_experts):
    T = top_e.shape[0]
    A = T * TOP_K
    e = top_e.reshape(A)
    onehot = (e[:, None] == jnp.arange(n_experts, dtype=jnp.int32)[None, :]).astype(jnp.int32)
    csum = jnp.cumsum(onehot, axis=0)
    rank = jnp.take_along_axis(csum, e[:, None], axis=1)[:, 0] - 1
    counts = csum[-1]
    padded = ((counts + EXPERT_TILE - 1) // EXPERT_TILE) * EXPERT_TILE
    pend = jnp.cumsum(padded)
    pstart = pend - padded
    dest = (pstart[e] + rank).astype(jnp.int32)
    n_slots = A + n_experts * EXPERT_TILE
    n_blocks = n_slots // EXPERT_TILE
    n_used = (pend[-1] // EXPERT_TILE).astype(jnp.int32)
    blk = jnp.arange(n_blocks, dtype=jnp.int32)
    be = jnp.sum((pend[None, :] <= (blk * EXPERT_TILE)[:, None]).astype(jnp.int32), axis=1)
    be = jnp.minimum(be, n_experts - 1)
    be = jnp.where(blk < n_used, be, be[jnp.maximum(n_used - 1, 0)])
    after = pend[be] // EXPERT_TILE
    nxt = jnp.where(after < n_used, be[jnp.minimum(after, n_blocks - 1)], -1).astype(jnp.int32)
    return dest.reshape(T, TOP_K), pend.astype(jnp.int32), be, nxt, n_used.reshape(1)


def _tile_major(dest, tm):
    T = dest.shape[0]
    return dest.reshape(T // tm, tm, TOP_K).transpose(0, 2, 1).reshape(T // tm, 1, TOP_K * tm)


def _issue_pairs(n, make_copy, wait=False):
    if not wait:
        for r in range(n):
            make_copy(r).start(priority=r % 2)
        return

    def body(u, c):
        make_copy(2 * u).wait()
        make_copy(2 * u + 1).wait()
        return c

    lax.fori_loop(0, n // 2, body, 0, unroll=4)


def _dispatch_kernel(pend_ref, dest_ref, h_ref, xs_hbm, zbuf, sem):
    i = pl.program_id(0)
    tm = dest_ref.shape[1] // TOP_K
    tile = zbuf.shape[0]
    ne = pend_ref.shape[0]

    @pl.when(i == 0)
    def _():
        zbuf[...] = jnp.zeros(zbuf.shape, zbuf.dtype)

        def fill(e):
            end = pend_ref[e]
            start = jnp.where(e == 0, 0, pend_ref[jnp.maximum(e - 1, 0)])
            last = pl.multiple_of(jnp.maximum(end - tile, 0), tile)
            return end > start, pltpu.make_async_copy(zbuf, xs_hbm.at[pl.ds(last, tile)], sem)

        def start(e, c):
            ok, cp = fill(e)

            @pl.when(ok)
            def _():
                cp.start()
            return c

        def wait(e, c):
            ok, cp = fill(e)

            @pl.when(ok)
            def _():
                cp.wait()
            return c

        lax.fori_loop(0, ne, start, 0)
        lax.fori_loop(0, ne, wait, 0)

        def tail(b):
            off = pl.multiple_of(b * tile, tile)
            return off >= pend_ref[ne - 1], pltpu.make_async_copy(zbuf, xs_hbm.at[pl.ds(off, tile)], sem)

        def tail_start(b, c):
            ok, cp = tail(b)

            @pl.when(ok)
            def _():
                cp.start()
            return c

        def tail_wait(b, c):
            ok, cp = tail(b)

            @pl.when(ok)
            def _():
                cp.wait()
            return c

        n_tiles = xs_hbm.shape[0] // tile
        lax.fori_loop(0, n_tiles, tail_start, 0)
        lax.fori_loop(0, n_tiles, tail_wait, 0)

    def row_copy(r):
        return pltpu.make_async_copy(h_ref.at[pl.ds(r % tm, 1)], xs_hbm.at[pl.ds(dest_ref[0, r], 1)], sem)

    _issue_pairs(TOP_K * tm, row_copy)
    _issue_pairs(TOP_K * tm, row_copy, wait=True)


def _dispatch(h2, dest, pend, n_slots, tm=128):
    T, D = h2.shape
    grid_spec = pltpu.PrefetchScalarGridSpec(
        num_scalar_prefetch=1,
        grid=(T // tm,),
        in_specs=[
            pl.BlockSpec((None, 1, TOP_K * tm), lambda i, pe: (i, 0, 0), memory_space=pltpu.SMEM),
            pl.BlockSpec((tm, D), lambda i, pe: (i, 0)),
        ],
        out_specs=pl.BlockSpec(memory_space=pl.ANY),
        scratch_shapes=[pltpu.VMEM((EXPERT_TILE, D), h2.dtype), pltpu.SemaphoreType.DMA(())],
    )
    return pl.pallas_call(
        _dispatch_kernel,
        grid_spec=grid_spec,
        out_shape=jax.ShapeDtypeStruct((n_slots, D), h2.dtype),
        compiler_params=_cparams(1),
        name="moe_dispatch",
    )(pend, _tile_major(dest, tm), h2)


def _expert_kernel(be_ref, nxt_ref, nused_ref, x_ref, wu_hbm, bu_ref, wd_hbm, bd_ref, o_ref,
                   wu32, wd32, wub, wdb, sems, *, layer):
    i = pl.program_id(0)
    d_ff = wd32.shape[0]

    def fetch(e):
        return (pltpu.make_async_copy(wu_hbm.at[layer, e], wu32, sems.at[0]),
                pltpu.make_async_copy(wd_hbm.at[layer, e], wd32, sems.at[1]))

    @pl.when(i == 0)
    def _():
        for cp in fetch(be_ref[0]):
            cp.start()

    @pl.when(i < nused_ref[0])
    def _():
        @pl.when((i == 0) | (be_ref[i] != be_ref[jnp.maximum(i - 1, 0)]))
        def _():
            for cp in fetch(be_ref[i]):
                cp.wait()
            _cast_rows(wu32, wub)
            _cast_rows(wd32, wdb)

            @pl.when(nxt_ref[i] >= 0)
            def _():
                for cp in fetch(nxt_ref[i]):
                    cp.start()

        x = x_ref[...].astype(jnp.bfloat16)
        u = jnp.dot(x, wub[...], preferred_element_type=jnp.float32) + bu_ref[...]
        glu = jnp.minimum(u[:, :d_ff], SWIGLU_LIMIT)
        lin = jnp.clip(u[:, d_ff:], -SWIGLU_LIMIT, SWIGLU_LIMIT)
        act = glu * (1.0 / (1.0 + jnp.exp(-SWIGLU_ALPHA * glu))) * (lin + 1.0)
        o_ref[...] = jnp.dot(act.astype(jnp.bfloat16), wdb[...], preferred_element_type=jnp.float32) + bd_ref[...]

    @pl.when(i >= nused_ref[0])
    def _():
        o_ref[...] = jnp.zeros(o_ref.shape, o_ref.dtype)


def _experts(xs, be, nxt, n_used, layer, w_up, b_up, w_down, b_down):
    n_slots, D = xs.shape
    depth, ne, _, f2 = w_up.shape
    d_ff = w_down.shape[2]
    tm = EXPERT_TILE
    n_blocks = n_slots // tm
    grid_spec = pltpu.PrefetchScalarGridSpec(
        num_scalar_prefetch=3,
        grid=(n_blocks,),
        in_specs=[
            pl.BlockSpec((tm, D), lambda i, be, nx, nu: (jnp.minimum(i, nu[0] - 1), 0)),
            pl.BlockSpec(memory_space=pl.ANY),
            pl.BlockSpec((None, None, 1, f2), lambda i, be, nx, nu: (layer, be[i], 0, 0)),
            pl.BlockSpec(memory_space=pl.ANY),
            pl.BlockSpec((None, None, 1, D), lambda i, be, nx, nu: (layer, be[i], 0, 0)),
        ],
        out_specs=pl.BlockSpec((tm, D), lambda i, be, nx, nu: (i, 0)),
        scratch_shapes=[
            pltpu.VMEM((D, f2), jnp.float32),
            pltpu.VMEM((d_ff, D), jnp.float32),
            pltpu.VMEM((D, f2), jnp.bfloat16),
            pltpu.VMEM((d_ff, D), jnp.bfloat16),
            pltpu.SemaphoreType.DMA((2,)),
        ],
    )
    return pl.pallas_call(
        functools.partial(_expert_kernel, layer=layer),
        grid_spec=grid_spec,
        out_shape=jax.ShapeDtypeStruct((n_slots, D), jnp.float32),
        compiler_params=_cparams(1),
        name="experts",
    )(be, nxt, n_used, xs, w_up, b_up.reshape(depth, ne, 1, f2), w_down, b_down.reshape(depth, ne, 1, D))


def _combine_kernel(dest_ref, y_hbm, x_ref, gate_ref, tg_ref, o_ref, ybuf, sem):
    tm = x_ref.shape[0]

    def row_copy(r):
        return pltpu.make_async_copy(y_hbm.at[pl.ds(dest_ref[0, r], 1)], ybuf.at[pl.ds(r, 1)], sem)

    _issue_pairs(TOP_K * tm, row_copy)
    _issue_pairs(TOP_K * tm, row_copy, wait=True)
    tg = tg_ref[...]
    y = None
    for k in range(TOP_K):
        part = tg[:, k:k + 1] * ybuf[k * tm:(k + 1) * tm, :]
        y = part if y is None else y + part
    o_ref[...] = x_ref[...] + gate_ref[...] * y


def _combine(outs, dest, tg, x1, mod6, tm=128):
    B, S, D = x1.shape
    nt = S // tm
    tok = pl.BlockSpec((None, tm, D), lambda b, i: (b, i, 0))
    return pl.pallas_call(
        _combine_kernel,
        grid=(B, nt),
        in_specs=[
            pl.BlockSpec((None, 1, TOP_K * tm), lambda b, i: (b * nt + i, 0, 0), memory_space=pltpu.SMEM),
            pl.BlockSpec(memory_space=pl.ANY),
            tok,
            pl.BlockSpec((None, None, 1, D), lambda b, i: (b, 5, 0, 0)),
            pl.BlockSpec((None, tm, LANES), lambda b, i: (b, i, 0)),
        ],
        out_specs=tok,
        out_shape=jax.ShapeDtypeStruct((B, S, D), jnp.float32),
        scratch_shapes=[pltpu.VMEM((TOP_K * tm, D), jnp.float32), pltpu.SemaphoreType.DMA(())],
        compiler_params=_cparams(2),
        name="moe_combine",
    )(_tile_major(dest, tm), outs, x1, mod6, tg)


def _final_kernel(x_ref, g_ref, o_ref):
    x = x_ref[...]
    ms = jnp.mean(x * x, axis=-1, keepdims=True)
    o_ref[...] = x * lax.rsqrt(ms + NORM_EPS) * g_ref[...]


def _final_norm(x, g_final, tm=512):
    B, S, D = x.shape
    tok = pl.BlockSpec((None, tm, D), lambda b, i: (b, i, 0))
    return pl.pallas_call(
        _final_kernel,
        grid=(B, S // tm),
        in_specs=[tok, pl.BlockSpec((1, D), lambda b, i: (0, 0))],
        out_specs=tok,
        out_shape=jax.ShapeDtypeStruct((B, S, D), jnp.float32),
        compiler_params=_cparams(2),
        name="final_norm",
    )(x, g_final.reshape(1, D))


def _rotary_tables(positions, scale_q):
    inv_freq = 1.0 / (ROPE_THETA ** (jnp.arange(0, HEAD_DIM, 2, dtype=jnp.float32) / HEAD_DIM))
    ang = positions[..., None].astype(jnp.float32) * inv_freq
    cos, sin = jnp.cos(ang), jnp.sin(ang)
    reps = LANES // HEAD_DIM
    cos_l = jnp.tile(jnp.concatenate([cos, cos], axis=-1), (1, 1, reps))
    sin_l = jnp.tile(jnp.concatenate([-sin, sin], axis=-1), (1, 1, reps))
    cos_t = jnp.stack([cos_l * scale_q, cos_l, jnp.ones_like(cos_l)])
    sin_t = jnp.stack([sin_l * scale_q, sin_l, jnp.zeros_like(sin_l)])
    return cos_t, sin_t


def kernel(x, c, positions, w_ada, b_ada, w_in, beta_a, beta_b, w_out, w_router, b_router, w_up, b_up, w_down, b_down, g_final):
    B, S, D = x.shape
    depth = w_ada.shape[0]
    ne = w_router.shape[-1]
    biases = tuple(jnp.asarray(b) for b in _dilated_biases())
    own_bias = jnp.asarray(_own_bias())

    xp = _to_perm(x)
    pos_p = _to_perm(positions)
    cos_t, sin_t = _rotary_tables(pos_p, HEAD_DIM ** -0.5)
    mod = _ada_mod(c, w_ada, b_ada)

    for l in range(depth):
        mod6 = mod[l, :B].reshape(B, 6, 1, D)
        qkv = _qkv_proj(xp, mod6, w_in, l, cos_t, sin_t)
        oa = _dilated_attention(qkv, biases)
        ob = _moba_attention(qkv, own_bias)
        beta = jnp.concatenate([beta_a[l], beta_b[l]]).reshape(1, D)
        x1, h2, te, tg = _post_attention(oa, ob, xp, mod6, beta, w_out, l, w_router[l], b_router[l])
        dest, pend, be, nxt, n_used = _route(te.reshape(B * S, LANES)[:, :TOP_K], ne)
        xs = _dispatch(h2.reshape(B * S, D), dest, pend, be.shape[0] * EXPERT_TILE)
        outs = _experts(xs, be, nxt, n_used, l, w_up, b_up, w_down, b_down)
        xp = _combine(outs, dest, tg, x1, mod6)

    return _from_perm(_final_norm(xp, g_final))
```

```python
import functools
import math

import numpy as np
import jax
import jax.numpy as jnp
from jax import lax
from jax.experimental import pallas as pl
from jax.experimental.pallas import tpu as pltpu

HEAD_DIM = 64
LANES = 128
BLK = 256
BAND = 128
DILATIONS = (1, 4, 16)
MOBA_TOPK = 3
TOP_K = 4
SWIGLU_LIMIT = 7.0
SWIGLU_ALPHA = 1.702
ROPE_THETA = 10000.0
NORM_EPS = 1e-6
NEG = -1e30
EXPERT_TILE = 256
VMEM_LIMIT = 60000 * 1024

_ARB = pltpu.ARBITRARY


def _cparams(n_axes):
    return pltpu.CompilerParams(dimension_semantics=(_ARB,) * n_axes, vmem_limit_bytes=VMEM_LIMIT)


def _to_perm(x):
    B, S = x.shape[:2]
    rest = x.shape[2:]
    x = x.reshape(B, S // BLK, 16, 4, 4, *rest)
    x = jnp.swapaxes(x, 2, 4)
    return x.reshape(B, S, *rest)


def _from_perm(x):
    B, S = x.shape[:2]
    rest = x.shape[2:]
    x = x.reshape(B, S // BLK, 4, 4, 16, *rest)
    x = jnp.swapaxes(x, 2, 4)
    return x.reshape(B, S, *rest)


def _local_pos():
    p = np.arange(BLK)
    return 16 * (p % 16) + 4 * ((p // 16) % 4) + p // 64


def _band_bias(seq_q, seq_k):
    dist = seq_q[:, None] - seq_k[None, :]
    ok = (dist >= 0) & (dist <= BAND)
    half = seq_k.shape[0] // 2
    exists = np.arange(seq_k.shape[0])[None, :] >= half
    normal = np.where(ok, 0.0, NEG).astype(np.float32)
    first = np.where(ok & exists, 0.0, NEG).astype(np.float32)
    two = np.stack([first, normal])
    return np.concatenate([two, two], axis=1)


def _dilated_biases():
    lp = _local_pos()
    b1 = _band_bias(lp + BLK, np.concatenate([lp, lp + BLK]))
    hi, c, lo = np.meshgrid(np.arange(4), np.arange(4), np.arange(16), indexing="ij")
    seq_k4 = (64 * hi + 4 * lo + c).reshape(-1)
    seq_q4 = seq_k4[:128] + 128
    b4 = _band_bias(seq_q4, seq_k4)
    b16 = _band_bias(np.arange(128) + 128, np.arange(256))
    return b1, b4, b16


def _own_bias():
    lp = _local_pos()
    c = np.where(lp[None, :] <= lp[:, None], 0.0, NEG).astype(np.float32)
    z = np.zeros_like(c)
    one_head = np.block([[c, z], [z, c]])
    return np.concatenate([one_head, one_head], axis=0)


def _stack_heads(q):
    lane = lax.broadcasted_iota(jnp.int32, q.shape, 1)
    zero = jnp.zeros_like(q)
    return jnp.concatenate([jnp.where(lane < HEAD_DIM, q, zero), jnp.where(lane >= HEAD_DIM, q, zero)], axis=0)


def _scores(qs, kk):
    return lax.dot_general(qs, kk, (((1,), (1,)), ((), ())), preferred_element_type=jnp.float32)


def _tile_lanes(x, k):
    return x if k == 1 else jnp.concatenate([x] * k, axis=1)


def _softmax_step(s, vv, m_old, l_old, acc_old):
    m_cur = jnp.max(s, axis=1, keepdims=True)
    m_new = jnp.maximum(m_old, m_cur)
    alpha = jnp.exp(m_old - m_new)
    p = jnp.exp(s - _tile_lanes(m_new, s.shape[1] // LANES))
    l_new = alpha * l_old + jnp.sum(p, axis=1, keepdims=True)
    pv = jnp.dot(p.astype(vv.dtype), vv, preferred_element_type=jnp.float32)
    return m_new, l_new, alpha * acc_old + pv


def _merge_heads(x):
    n = x.shape[0] // 2
    lane = lax.broadcasted_iota(jnp.int32, (n, LANES), 1)
    return jnp.where(lane < HEAD_DIM, x[:n], x[n:])


def _ada_kernel(c_ref, w_ref, b_ref, o_ref):
    c = c_ref[...]
    sc = (c * (1.0 / (1.0 + jnp.exp(-c)))).astype(jnp.bfloat16)
    o_ref[...] = jnp.dot(sc, w_ref[...].astype(jnp.bfloat16), preferred_element_type=jnp.float32) + b_ref[...]


def _ada_mod(c, w_ada, b_ada):
    depth, d, n = w_ada.shape
    rows = 8
    tn = math.gcd(n, 1024)
    cp = jnp.zeros((rows, d), jnp.float32).at[: c.shape[0]].set(c)
    return pl.pallas_call(
        _ada_kernel,
        grid=(depth, n // tn),
        in_specs=[
            pl.BlockSpec((rows, d), lambda l, j: (0, 0)),
            pl.BlockSpec((None, d, tn), lambda l, j: (l, 0, j)),
            pl.BlockSpec((None, 1, tn), lambda l, j: (l, 0, j)),
        ],
        out_specs=pl.BlockSpec((None, rows, tn), lambda l, j: (l, 0, j)),
        out_shape=jax.ShapeDtypeStruct((depth, rows, n), jnp.float32),
        compiler_params=_cparams(2),
        name="ada_mod",
    )(cp, w_ada, b_ada.reshape(depth, 1, n))


def _rms_mod(x, scale, shift):
    ms = jnp.mean(x * x, axis=-1, keepdims=True)
    return x * lax.rsqrt(ms + NORM_EPS) * (1.0 + scale) + shift


def _cast_rows(src_ref, dst_ref):
    rows_per_step = math.gcd(src_ref.shape[0], 256)
    n = src_ref.shape[0] // rows_per_step

    def body(i, carry):
        r = pl.multiple_of(i * rows_per_step, rows_per_step)
        dst_ref[pl.ds(r, rows_per_step), :] = src_ref[pl.ds(r, rows_per_step), :].astype(dst_ref.dtype)
        return carry

    lax.fori_loop(0, n, body, 0)


def _qkv_kernel(x_ref, shift_ref, scale_ref, w_ref, cos_ref, sin_ref, o_ref, wbf_ref):
    b = pl.program_id(1)
    i = pl.program_id(2)

    @pl.when((b == 0) & (i == 0))
    def _():
        _cast_rows(w_ref, wbf_ref)

    h = _rms_mod(x_ref[...], scale_ref[...], shift_ref[...]).astype(jnp.bfloat16)
    acc = jnp.dot(h, wbf_ref[...], preferred_element_type=jnp.float32)
    cos = cos_ref[...]
    sin = sin_ref[...]
    lane = lax.broadcasted_iota(jnp.int32, cos.shape, 1)
    first_half = (lane % HEAD_DIM) < (HEAD_DIM // 2)
    for c in range(o_ref.shape[0]):
        a = acc[:, c * LANES:(c + 1) * LANES]
        partner = jnp.where(first_half, pltpu.roll(a, LANES - HEAD_DIM // 2, 1), pltpu.roll(a, HEAD_DIM // 2, 1))
        o_ref[c] = (a * cos + partner * sin).astype(o_ref.dtype)


def _qkv_proj(x, mod6, w_in, layer, cos_t, sin_t, tm=512):
    B, S, D = x.shape
    n3 = w_in.shape[2]
    width = n3 // 3
    pairs = width // LANES
    return pl.pallas_call(
        _qkv_kernel,
        grid=(3, B, S // tm),
        in_specs=[
            pl.BlockSpec((None, tm, D), lambda j, b, i: (b, i, 0)),
            pl.BlockSpec((None, None, 1, D), lambda j, b, i: (b, 0, 0, 0)),
            pl.BlockSpec((None, None, 1, D), lambda j, b, i: (b, 1, 0, 0)),
            pl.BlockSpec((None, D, width), lambda j, b, i: (layer, 0, j), pipeline_mode=pl.Buffered(1)),
            pl.BlockSpec((None, None, tm, LANES), lambda j, b, i: (j, b, i, 0)),
            pl.BlockSpec((None, None, tm, LANES), lambda j, b, i: (j, b, i, 0)),
        ],
        out_specs=pl.BlockSpec((None, pairs, tm, LANES), lambda j, b, i: (b, j, i, 0)),
        out_shape=jax.ShapeDtypeStruct((B, 3 * pairs, S, LANES), jnp.bfloat16),
        scratch_shapes=[pltpu.VMEM((D, width), jnp.bfloat16)],
        compiler_params=_cparams(3),
        name="qkv_proj",
    )(x, mod6, mod6, w_in, cos_t, sin_t)


def _dilated_kernel(q_ref, k_ref, v_ref, b1_ref, b4_ref, b16_ref, o_ref, m_ref, l_ref, acc_ref):
    nb = q_ref.shape[0]

    def rows(x):
        return x.reshape(-1, LANES)

    def update(items):
        old = [(acc_ref[idx], m_ref[(0,) + idx], m_ref[(1,) + idx], l_ref[(0,) + idx], l_ref[(1,) + idx])
               for (_, _, _, _, idx) in items]
        new = []
        for (q, kk, vv, bias, idx), (a_raw, m0, m1, l0, l1) in zip(items, old):
            s = _scores(_stack_heads(q), kk) + bias
            m_old = jnp.concatenate([rows(m0), rows(m1)], axis=0)
            l_old = jnp.concatenate([rows(l0), rows(l1)], axis=0)
            a_old = jnp.concatenate([rows(a_raw), rows(a_raw)], axis=0)
            new.append(_softmax_step(s, vv, m_old, l_old, a_old) + (a_raw.shape, q.shape[0]))
        for (_, _, _, _, idx), (m_new, l_new, a_new, shp, n) in zip(items, new):
            m_ref[(0,) + idx] = m_new[:n].reshape(shp)
            m_ref[(1,) + idx] = m_new[n:].reshape(shp)
            l_ref[(0,) + idx] = l_new[:n].reshape(shp)
            l_ref[(1,) + idx] = l_new[n:].reshape(shp)
            acc_ref[idx] = _merge_heads(a_new).reshape(shp)

    def init(n, carry):
        m_ref[0, n] = jnp.full(m_ref.shape[2:], NEG, jnp.float32)
        m_ref[1, n] = jnp.full(m_ref.shape[2:], NEG, jnp.float32)
        l_ref[0, n] = jnp.zeros(l_ref.shape[2:], jnp.float32)
        l_ref[1, n] = jnp.zeros(l_ref.shape[2:], jnp.float32)
        acc_ref[n] = jnp.zeros(acc_ref.shape[1:], jnp.float32)
        return carry

    lax.fori_loop(0, nb, init, 0)

    def run(n_blocks, group, make_item):
        def step(t, carry):
            update([make_item(group * t + g) for g in range(group)])
            return carry

        lax.fori_loop(0, n_blocks // group, step, 0)

    def d1(n):
        prev = jnp.maximum(n - 1, 0)
        kk = jnp.concatenate([rows(k_ref[prev]), rows(k_ref[n])], axis=0)
        vv = jnp.concatenate([rows(v_ref[prev]), rows(v_ref[n])], axis=0)
        return rows(q_ref[n]), kk, vv, b1_ref[jnp.minimum(n, 1)], (n,)

    run(nb, 4, d1)

    per4 = nb // 2

    def d4(t):
        r4 = t // per4
        a = t % per4
        cur = pl.ds(2 * a, 2)
        prev = pl.ds(jnp.maximum(2 * a - 2, 0), 2)
        grp = pl.ds(4 * r4, 4)
        kk = jnp.concatenate([rows(k_ref[prev, grp]), rows(k_ref[cur, grp])], axis=0)
        vv = jnp.concatenate([rows(v_ref[prev, grp]), rows(v_ref[cur, grp])], axis=0)
        return rows(q_ref[cur, grp]), kk, vv, b4_ref[jnp.minimum(a, 1)], (cur, grp)

    run(4 * per4, 8, d4)

    per16 = nb // 8

    def d16(t):
        rr = t // per16
        b = t % per16
        cur = pl.ds(8 * b, 8)
        prev = pl.ds(jnp.maximum(8 * b - 8, 0), 8)
        kk = jnp.concatenate([rows(k_ref[prev, rr]), rows(k_ref[cur, rr])], axis=0)
        vv = jnp.concatenate([rows(v_ref[prev, rr]), rows(v_ref[cur, rr])], axis=0)
        return rows(q_ref[cur, rr]), kk, vv, b16_ref[jnp.minimum(b, 1)], (cur, rr)

    run(16 * per16, 8, d16)

    def fin(n, carry):
        lane = lax.broadcasted_iota(jnp.int32, acc_ref.shape[1:], 2)
        den = jnp.where(lane < HEAD_DIM, l_ref[0, n], l_ref[1, n])
        o_ref[n] = (acc_ref[n] / den).astype(o_ref.dtype)
        return carry

    lax.fori_loop(0, nb, fin, 0)


def _seq_view(a):
    B, P, S, L = a.shape
    return a.reshape(B, P, S // BLK, 16, 16, L)


def _dilated_attention(qkv, biases):
    B, P3, S, _ = qkv.shape
    P = P3 // 3
    pa = P // 2
    nb = S // BLK
    assert S % (16 * BAND) == 0, "sequence must be a whole number of dilation-16 band blocks"
    v6 = _seq_view(qkv)
    blk = (None, None, nb, 16, 16, LANES)
    b1, b4, b16 = biases
    out = pl.pallas_call(
        _dilated_kernel,
        grid=(B, pa),
        in_specs=[
            pl.BlockSpec(blk, lambda b, p: (b, p, 0, 0, 0, 0)),
            pl.BlockSpec(blk, lambda b, p: (b, P + p, 0, 0, 0, 0)),
            pl.BlockSpec(blk, lambda b, p: (b, 2 * P + p, 0, 0, 0, 0)),
            pl.BlockSpec(b1.shape, lambda b, p: (0, 0, 0)),
            pl.BlockSpec(b4.shape, lambda b, p: (0, 0, 0)),
            pl.BlockSpec(b16.shape, lambda b, p: (0, 0, 0)),
        ],
        out_specs=pl.BlockSpec(blk, lambda b, p: (b, p, 0, 0, 0, 0)),
        out_shape=jax.ShapeDtypeStruct((B, pa, nb, 16, 16, LANES), jnp.bfloat16),
        scratch_shapes=[
            pltpu.VMEM((2, nb, 16, 16, LANES), jnp.float32),
            pltpu.VMEM((2, nb, 16, 16, LANES), jnp.float32),
            pltpu.VMEM((nb, 16, 16, LANES), jnp.float32),
        ],
        compiler_params=_cparams(2),
        name="dilated_attn",
    )(v6, v6, v6, b1, b4, b16)
    return out.reshape(B, pa, S, LANES)


def _moba_kernel(q_ref, k_ref, v_ref, own_ref, hot_ref, o_ref, kmean_ref, qa_ref, sa_ref, sb_ref, m_ref, l_ref, acc_ref):
    m = pl.program_id(2)
    nb = k_ref.shape[0]
    nbp = -(-nb // 8) * 8

    def rows(x):
        return x.reshape(-1, LANES)

    @pl.when(m == 0)
    def _():
        kmean_ref[...] = jnp.zeros(kmean_ref.shape, jnp.float32)
        for j in range(nb):
            kmean_ref[pl.ds(j, 1), :] = jnp.mean(rows(k_ref[j]).astype(jnp.float32), axis=0, keepdims=True)

    qs = _stack_heads(rows(q_ref[...]))
    nq = qs.shape[0]

    gate = _scores(kmean_ref[...].astype(jnp.bfloat16), qs)[:nbp]
    jidx = lax.broadcasted_iota(jnp.int32, gate.shape, 0)
    own = 2 * m + (lax.broadcasted_iota(jnp.int32, gate.shape, 1) // BLK) % 2
    g = jnp.where(jidx < own, gate, -jnp.inf)
    rank = jnp.zeros(gate.shape, jnp.int32)
    for jp in range(nb):
        other = g[jp:jp + 1, :]
        beats = (other > g) | ((other == g) & (jp < jidx))
        rank = rank + beats.astype(jnp.int32)
    keep = ((rank < MOBA_TOPK) & (jidx < own)) | (jidx == own)
    pen_t = jnp.where(keep, 0.0, NEG).astype(jnp.float32)
    pen_t = jnp.concatenate([pen_t, jnp.full((LANES - nbp, nq), NEG, jnp.float32)], axis=0)
    qa_ref[...] = jnp.concatenate([qs, pen_t.T.astype(qs.dtype)], axis=1)

    def pair_scores(t):
        j0 = jnp.minimum(2 * t, nb - 2)
        j1 = j0 + 1
        kk = jnp.concatenate([jnp.concatenate([rows(k_ref[j0]), hot_ref[j0]], axis=1),
                              jnp.concatenate([rows(k_ref[j1]), hot_ref[j1]], axis=1)], axis=0)
        return _scores(qa_ref[...], kk)

    def pair_values(t):
        return jnp.concatenate([rows(v_ref[2 * t]), rows(v_ref[2 * t + 1])], axis=0)

    def absorb(s, t):
        m_new, l_new, a_new = _softmax_step(s, pair_values(t), m_ref[...], l_ref[...], acc_ref[...])
        m_ref[...] = m_new
        l_ref[...] = l_new
        acc_ref[...] = a_new

    s = pair_scores(m) + own_ref[...]
    m0 = jnp.max(s, axis=1, keepdims=True)
    p = jnp.exp(s - m0)
    m_ref[...] = jnp.broadcast_to(m0, m_ref.shape)
    l_ref[...] = jnp.broadcast_to(jnp.sum(p, axis=1, keepdims=True), l_ref.shape)
    acc_ref[...] = jnp.dot(p.astype(jnp.bfloat16), pair_values(m), preferred_element_type=jnp.float32)

    sa_ref[...] = pair_scores(0)

    def past(u, carry):
        t = 2 * u
        sb_ref[...] = pair_scores(t + 1)
        absorb(sa_ref[...], t)
        sa_ref[...] = pair_scores(t + 2)
        absorb(sb_ref[...], t + 1)
        return carry

    lax.fori_loop(0, m // 2, past, 0)

    @pl.when(m % 2 == 1)
    def _():
        absorb(sa_ref[...], m - 1)

    o_ref[...] = _merge_heads(acc_ref[...] / l_ref[...]).reshape(o_ref.shape).astype(o_ref.dtype)


def _block_one_hot(nb):
    hot = np.zeros((nb, BLK, LANES), np.float32)
    hot[np.arange(nb), :, np.arange(nb)] = 1.0
    return jnp.asarray(hot, jnp.bfloat16)


def _moba_attention(qkv, own_bias):
    B, P3, S, _ = qkv.shape
    P = P3 // 3
    pa = P // 2
    nb = S // BLK
    assert nb <= LANES
    v6 = _seq_view(qkv)
    hot = _block_one_hot(nb)
    full = (None, None, nb, 16, 16, LANES)
    tile = (None, None, 2, 16, 16, LANES)
    nq = 4 * BLK
    out = pl.pallas_call(
        _moba_kernel,
        grid=(B, pa, nb // 2),
        in_specs=[
            pl.BlockSpec(tile, lambda b, p, n: (b, pa + p, n, 0, 0, 0)),
            pl.BlockSpec(full, lambda b, p, n: (b, P + pa + p, 0, 0, 0, 0)),
            pl.BlockSpec(full, lambda b, p, n: (b, 2 * P + pa + p, 0, 0, 0, 0)),
            pl.BlockSpec(own_bias.shape, lambda b, p, n: (0, 0)),
            pl.BlockSpec(hot.shape, lambda b, p, n: (0, 0, 0)),
        ],
        out_specs=pl.BlockSpec(tile, lambda b, p, n: (b, p, n, 0, 0, 0)),
        out_shape=jax.ShapeDtypeStruct((B, pa, nb, 16, 16, LANES), jnp.bfloat16),
        scratch_shapes=[
            pltpu.VMEM((LANES, LANES), jnp.float32),
            pltpu.VMEM((nq, 2 * LANES), jnp.bfloat16),
            pltpu.VMEM((nq, 2 * BLK), jnp.float32),
            pltpu.VMEM((nq, 2 * BLK), jnp.float32),
            pltpu.VMEM((nq, LANES), jnp.float32),
            pltpu.VMEM((nq, LANES), jnp.float32),
            pltpu.VMEM((nq, LANES), jnp.float32),
        ],
        compiler_params=_cparams(3),
        name="moba_attn",
    )(v6, v6, v6, own_bias, hot)
    return out.reshape(B, pa, S, LANES)


def _post_kernel(oa_ref, ob_ref, x_ref, gate_a_ref, shift_f_ref, scale_f_ref, beta_ref, w_ref, wr_ref, br_ref,
                 x1_ref, h2_ref, te_ref, tg_ref, wbf_ref, y_ref, whi_ref, wlo_ref):
    b = pl.program_id(0)
    i = pl.program_id(1)

    @pl.when((b == 0) & (i == 0))
    def _():
        _cast_rows(w_ref, wbf_ref)
        wr = wr_ref[...]
        hi = wr.astype(jnp.bfloat16)
        whi_ref[...] = hi
        wlo_ref[...] = (wr - hi.astype(jnp.float32)).astype(jnp.bfloat16)

    half = oa_ref.shape[0] * LANES
    for off, o_ref in ((0, oa_ref), (half, ob_ref)):
        ss = None
        for c in range(o_ref.shape[0]):
            o = o_ref[c].astype(jnp.float32)
            part = jnp.sum(o * o, axis=-1, keepdims=True)
            ss = part if ss is None else ss + part
        inv = lax.rsqrt(ss / half + NORM_EPS)
        for c in range(o_ref.shape[0]):
            lo = off + c * LANES
            y_ref[:, lo:lo + LANES] = (o_ref[c].astype(jnp.float32) * inv * beta_ref[:, lo:lo + LANES]).astype(y_ref.dtype)

    mixed = jnp.dot(y_ref[...], wbf_ref[...], preferred_element_type=jnp.float32)
    x1 = x_ref[...] + gate_a_ref[...] * mixed
    x1_ref[...] = x1
    h2 = _rms_mod(x1, scale_f_ref[...], shift_f_ref[...])
    h2_ref[...] = h2

    h_hi = h2.astype(jnp.bfloat16)
    h_lo = (h2 - h_hi.astype(jnp.float32)).astype(jnp.bfloat16)
    logits = (jnp.dot(h_hi, whi_ref[...], preferred_element_type=jnp.float32)
              + jnp.dot(h_hi, wlo_ref[...], preferred_element_type=jnp.float32)
              + jnp.dot(h_lo, whi_ref[...], preferred_element_type=jnp.float32)) + br_ref[...]
    lane = lax.broadcasted_iota(jnp.int32, logits.shape, 1).astype(jnp.float32)
    te = jnp.zeros(logits.shape, jnp.float32)
    tg = jnp.zeros(logits.shape, jnp.float32)
    work = logits
    top0 = None
    den = None
    for kk in range(TOP_K):
        mv = jnp.max(work, axis=1, keepdims=True)
        idx = jnp.min(jnp.where(work == mv, lane, float(LANES)), axis=1, keepdims=True)
        work = jnp.where(lane == idx, -jnp.inf, work)
        if kk == 0:
            top0 = mv
        e = jnp.exp(mv - top0)
        den = e if den is None else den + e
        te = jnp.where(lane == kk, idx, te)
        tg = jnp.where(lane == kk, e, tg)
    te_ref[...] = te.astype(jnp.int32)
    tg_ref[...] = tg / den


def _post_attention(oa, ob, x, mod6, beta, w_out, layer, w_router, b_router, tm=256):
    B, S, D = x.shape
    pa = oa.shape[1]
    ne = w_router.shape[1]
    assert TOP_K <= ne <= LANES
    w_router = jnp.pad(w_router, ((0, 0), (0, LANES - ne)))
    b_router = jnp.pad(b_router, (0, LANES - ne), constant_values=-jnp.inf)
    ne = LANES
    row = lambda k: pl.BlockSpec((None, None, 1, D), lambda b, i, k=k: (b, k, 0, 0))
    tok = pl.BlockSpec((None, tm, D), lambda b, i: (b, i, 0))
    small = pl.BlockSpec((None, tm, LANES), lambda b, i: (b, i, 0))
    return pl.pallas_call(
        _post_kernel,
        grid=(B, S // tm),
        in_specs=[
            pl.BlockSpec((None, pa, tm, LANES), lambda b, i: (b, 0, i, 0)),
            pl.BlockSpec((None, pa, tm, LANES), lambda b, i: (b, 0, i, 0)),
            tok, row(2), row(3), row(4),
            pl.BlockSpec((1, D), lambda b, i: (0, 0)),
            pl.BlockSpec((None,) + w_out.shape[1:], lambda b, i: (layer, 0, 0), pipeline_mode=pl.Buffered(1)),
            pl.BlockSpec(w_router.shape, lambda b, i: (0, 0)),
            pl.BlockSpec((1, ne), lambda b, i: (0, 0)),
        ],
        out_specs=[tok, tok, small, small],
        out_shape=[
            jax.ShapeDtypeStruct((B, S, D), jnp.float32),
            jax.ShapeDtypeStruct((B, S, D), jnp.float32),
            jax.ShapeDtypeStruct((B, S, LANES), jnp.int32),
            jax.ShapeDtypeStruct((B, S, LANES), jnp.float32),
        ],
        scratch_shapes=[pltpu.VMEM(w_out.shape[1:], jnp.bfloat16), pltpu.VMEM((tm, D), jnp.bfloat16),
                        pltpu.VMEM(w_router.shape, jnp.bfloat16), pltpu.VMEM(w_router.shape, jnp.bfloat16)],
        compiler_params=_cparams(2),
        name="post_attn_router",
    )(oa, ob, x, mod6, mod6, mod6, beta, w_out, w_router, b_router.reshape(1, ne))


def _route(top_e, n_experts):
    T = top_e.shape[0]
    A = T * TOP_K
    e = top_e.reshape(A)
    onehot = (e[:, None] == jnp.arange(n_experts, dtype=jnp.int32)[None, :]).astype(jnp.int32)
    csum = jnp.cumsum(onehot, axis=0)
    rank = jnp.take_along_axis(csum, e[:, None], axis=1)[:, 0] - 1
    counts = csum[-1]
    padded = ((counts + EXPERT_TILE - 1) // EXPERT_TILE) * EXPERT_TILE
    pend = jnp.cumsum(padded)
    pstart = pend - padded
    dest = (pstart[e] + rank).astype(jnp.int32)
    n_slots = A + n_experts * EXPERT_TILE
    n_blocks = n_slots // EXPERT_TILE
    n_used = (pend[-1] // EXPERT_TILE).astype(jnp.int32)
    blk = jnp.arange(n_blocks, dtype=jnp.int32)
    be = jnp.sum((pend[None, :] <= (blk * EXPERT_TILE)[:, None]).astype(jnp.int32), axis=1)
    be = jnp.minimum(be, n_experts - 1)
    be = jnp.where(blk < n_used, be, be[jnp.maximum(n_used - 1, 0)])
    after = pend[be] // EXPERT_TILE
    nxt = jnp.where(after < n_used, be[jnp.minimum(after, n_blocks - 1)], -1).astype(jnp.int32)
    return dest.reshape(T, TOP_K), pend.astype(jnp.int32), be, nxt, n_used.reshape(1)


def _tile_major(dest, tm):
    T = dest.shape[0]
    return dest.reshape(T // tm, tm, TOP_K).transpose(0, 2, 1).reshape(T // tm, 1, TOP_K * tm)


def _issue_pairs(n, make_copy, wait=False):
    if not wait:
        for r in range(n):
            make_copy(r).start(priority=r % 2)
        return

    def body(u, c):
        make_copy(2 * u).wait()
        make_copy(2 * u + 1).wait()
        return c

    lax.fori_loop(0, n // 2, body, 0, unroll=4)


def _dispatch_kernel(pend_ref, dest_ref, h_ref, xs_hbm, zbuf, sem):
    i = pl.program_id(0)
    tm = dest_ref.shape[1] // TOP_K
    tile = zbuf.shape[0]
    ne = pend_ref.shape[0]

    @pl.when(i == 0)
    def _():
        zbuf[...] = jnp.zeros(zbuf.shape, zbuf.dtype)

        def fill(e):
            end = pend_ref[e]
            start = jnp.where(e == 0, 0, pend_ref[jnp.maximum(e - 1, 0)])
            last = pl.multiple_of(jnp.maximum(end - tile, 0), tile)
            return end > start, pltpu.make_async_copy(zbuf, xs_hbm.at[pl.ds(last, tile)], sem)

        def start(e, c):
            ok, cp = fill(e)

            @pl.when(ok)
            def _():
                cp.start()
            return c

        def wait(e, c):
            ok, cp = fill(e)

            @pl.when(ok)
            def _():
                cp.wait()
            return c

        lax.fori_loop(0, ne, start, 0)
        lax.fori_loop(0, ne, wait, 0)

        def tail(b):
            off = pl.multiple_of(b * tile, tile)
            return off >= pend_ref[ne - 1], pltpu.make_async_copy(zbuf, xs_hbm.at[pl.ds(off, tile)], sem)

        def tail_start(b, c):
            ok, cp = tail(b)

            @pl.when(ok)
            def _():
                cp.start()
            return c

        def tail_wait(b, c):
            ok, cp = tail(b)

            @pl.when(ok)
            def _():
                cp.wait()
            return c

        n_tiles = xs_hbm.shape[0] // tile
        lax.fori_loop(0, n_tiles, tail_start, 0)
        lax.fori_loop(0, n_tiles, tail_wait, 0)

    def row_copy(r):
        return pltpu.make_async_copy(h_ref.at[pl.ds(r % tm, 1)], xs_hbm.at[pl.ds(dest_ref[0, r], 1)], sem)

    _issue_pairs(TOP_K * tm, row_copy)
    _issue_pairs(TOP_K * tm, row_copy, wait=True)


def _dispatch(h2, dest, pend, n_slots, tm=128):
    T, D = h2.shape
    grid_spec = pltpu.PrefetchScalarGridSpec(
        num_scalar_prefetch=1,
        grid=(T // tm,),
        in_specs=[
            pl.BlockSpec((None, 1, TOP_K * tm), lambda i, pe: (i, 0, 0), memory_space=pltpu.SMEM),
            pl.BlockSpec((tm, D), lambda i, pe: (i, 0)),
        ],
        out_specs=pl.BlockSpec(memory_space=pl.ANY),
        scratch_shapes=[pltpu.VMEM((EXPERT_TILE, D), h2.dtype), pltpu.SemaphoreType.DMA(())],
    )
    return pl.pallas_call(
        _dispatch_kernel,
        grid_spec=grid_spec,
        out_shape=jax.ShapeDtypeStruct((n_slots, D), h2.dtype),
        compiler_params=_cparams(1),
        name="moe_dispatch",
    )(pend, _tile_major(dest, tm), h2)


def _expert_kernel(be_ref, nxt_ref, nused_ref, x_ref, wu_hbm, bu_ref, wd_hbm, bd_ref, o_ref,
                   wu32, wd32, wub, wdb, sems, *, layer):
    i = pl.program_id(0)
    d_ff = wd32.shape[0]

    def fetch(e):
        return (pltpu.make_async_copy(wu_hbm.at[layer, e], wu32, sems.at[0]),
                pltpu.make_async_copy(wd_hbm.at[layer, e], wd32, sems.at[1]))

    @pl.when(i == 0)
    def _():
        for cp in fetch(be_ref[0]):
            cp.start()

    @pl.when(i < nused_ref[0])
    def _():
        @pl.when((i == 0) | (be_ref[i] != be_ref[jnp.maximum(i - 1, 0)]))
        def _():
            for cp in fetch(be_ref[i]):
                cp.wait()
            _cast_rows(wu32, wub)
            _cast_rows(wd32, wdb)

            @pl.when(nxt_ref[i] >= 0)
            def _():
                for cp in fetch(nxt_ref[i]):
                    cp.start()

        x = x_ref[...].astype(jnp.bfloat16)
        u = jnp.dot(x, wub[...], preferred_element_type=jnp.float32) + bu_ref[...]
        glu = jnp.minimum(u[:, :d_ff], SWIGLU_LIMIT)
        lin = jnp.clip(u[:, d_ff:], -SWIGLU_LIMIT, SWIGLU_LIMIT)
        act = glu * (1.0 / (1.0 + jnp.exp(-SWIGLU_ALPHA * glu))) * (lin + 1.0)
        o_ref[...] = jnp.dot(act.astype(jnp.bfloat16), wdb[...], preferred_element_type=jnp.float32) + bd_ref[...]

    @pl.when(i >= nused_ref[0])
    def _():
        o_ref[...] = jnp.zeros(o_ref.shape, o_ref.dtype)


def _experts(xs, be, nxt, n_used, layer, w_up, b_up, w_down, b_down):
    n_slots, D = xs.shape
    depth, ne, _, f2 = w_up.shape
    d_ff = w_down.shape[2]
    tm = EXPERT_TILE
    n_blocks = n_slots // tm
    grid_spec = pltpu.PrefetchScalarGridSpec(
        num_scalar_prefetch=3,
        grid=(n_blocks,),
        in_specs=[
            pl.BlockSpec((tm, D), lambda i, be, nx, nu: (jnp.minimum(i, nu[0] - 1), 0)),
            pl.BlockSpec(memory_space=pl.ANY),
            pl.BlockSpec((None, None, 1, f2), lambda i, be, nx, nu: (layer, be[i], 0, 0)),
            pl.BlockSpec(memory_space=pl.ANY),
            pl.BlockSpec((None, None, 1, D), lambda i, be, nx, nu: (layer, be[i], 0, 0)),
        ],
        out_specs=pl.BlockSpec((tm, D), lambda i, be, nx, nu: (i, 0)),
        scratch_shapes=[
            pltpu.VMEM((D, f2), jnp.float32),
            pltpu.VMEM((d_ff, D), jnp.float32),
            pltpu.VMEM((D, f2), jnp.bfloat16),
            pltpu.VMEM((d_ff, D), jnp.bfloat16),
            pltpu.SemaphoreType.DMA((2,)),
        ],
    )
    return pl.pallas_call(
        functools.partial(_expert_kernel, layer=layer),
        grid_spec=grid_spec,
        out_shape=jax.ShapeDtypeStruct((n_slots, D), jnp.float32),
        compiler_params=_cparams(1),
        name="experts",
    )(be, nxt, n_used, xs, w_up, b_up.reshape(depth, ne, 1, f2), w_down, b_down.reshape(depth, ne, 1, D))


def _combine_kernel(dest_ref, y_hbm, x_ref, gate_ref, tg_ref, o_ref, ybuf, sem):
    tm = x_ref.shape[0]

    def row_copy(r):
        return pltpu.make_async_copy(y_hbm.at[pl.ds(dest_ref[0, r], 1)], ybuf.at[pl.ds(r, 1)], sem)

    _issue_pairs(TOP_K * tm, row_copy)
    _issue_pairs(TOP_K * tm, row_copy, wait=True)
    tg = tg_ref[...]
    y = None
    for k in range(TOP_K):
        part = tg[:, k:k + 1] * ybuf[k * tm:(k + 1) * tm, :]
        y = part if y is None else y + part
    o_ref[...] = x_ref[...] + gate_ref[...] * y


def _combine(outs, dest, tg, x1, mod6, tm=128):
    B, S, D = x1.shape
    nt = S // tm
    tok = pl.BlockSpec((None, tm, D), lambda b, i: (b, i, 0))
    return pl.pallas_call(
        _combine_kernel,
        grid=(B, nt),
        in_specs=[
            pl.BlockSpec((None, 1, TOP_K * tm), lambda b, i: (b * nt + i, 0, 0), memory_space=pltpu.SMEM),
            pl.BlockSpec(memory_space=pl.ANY),
            tok,
            pl.BlockSpec((None, None, 1, D), lambda b, i: (b, 5, 0, 0)),
            pl.BlockSpec((None, tm, LANES), lambda b, i: (b, i, 0)),
        ],
        out_specs=tok,
        out_shape=jax.ShapeDtypeStruct((B, S, D), jnp.float32),
        scratch_shapes=[pltpu.VMEM((TOP_K * tm, D), jnp.float32), pltpu.SemaphoreType.DMA(())],
        compiler_params=_cparams(2),
        name="moe_combine",
    )(_tile_major(dest, tm), outs, x1, mod6, tg)


def _final_kernel(x_ref, g_ref, o_ref):
    x = x_ref[...]
    ms = jnp.mean(x * x, axis=-1, keepdims=True)
    o_ref[...] = x * lax.rsqrt(ms + NORM_EPS) * g_ref[...]


def _final_norm(x, g_final, tm=512):
    B, S, D = x.shape
    tok = pl.BlockSpec((None, tm, D), lambda b, i: (b, i, 0))
    return pl.pallas_call(
        _final_kernel,
        grid=(B, S // tm),
        in_specs=[tok, pl.BlockSpec((1, D), lambda b, i: (0, 0))],
        out_specs=tok,
        out_shape=jax.ShapeDtypeStruct((B, S, D), jnp.float32),
        compiler_params=_cparams(2),
        name="final_norm",
    )(x, g_final.reshape(1, D))


def _rotary_tables(positions, scale_q):
    inv_freq = 1.0 / (ROPE_THETA ** (jnp.arange(0, HEAD_DIM, 2, dtype=jnp.float32) / HEAD_DIM))
    ang = positions[..., None].astype(jnp.float32) * inv_freq
    cos, sin = jnp.cos(ang), jnp.sin(ang)
    reps = LANES // HEAD_DIM
    cos_l = jnp.tile(jnp.concatenate([cos, cos], axis=-1), (1, 1, reps))
    sin_l = jnp.tile(jnp.concatenate([-sin, sin], axis=-1), (1, 1, reps))
    cos_t = jnp.stack([cos_l * scale_q, cos_l, jnp.ones_like(cos_l)])
    sin_t = jnp.stack([sin_l * scale_q, sin_l, jnp.zeros_like(sin_l)])
    return cos_t, sin_t


def kernel(x, c, positions, w_ada, b_ada, w_in, beta_a, beta_b, w_out, w_router, b_router, w_up, b_up, w_down, b_down, g_final):
    B, S, D = x.shape
    depth = w_ada.shape[0]
    ne = w_router.shape[-1]
    biases = tuple(jnp.asarray(b) for b in _dilated_biases())
    own_bias = jnp.asarray(_own_bias())

    xp = _to_perm(x)
    pos_p = _to_perm(positions)
    cos_t, sin_t = _rotary_tables(pos_p, HEAD_DIM ** -0.5)
    mod = _ada_mod(c, w_ada, b_ada)

    for l in range(depth):
        mod6 = mod[l, :B].reshape(B, 6, 1, D)
        qkv = _qkv_proj(xp, mod6, w_in, l, cos_t, sin_t)
        oa = _dilated_attention(qkv, biases)
        ob = _moba_attention(qkv, own_bias)
        beta = jnp.concatenate([beta_a[l], beta_b[l]]).reshape(1, D)
        x1, h2, te, tg = _post_attention(oa, ob, xp, mod6, beta, w_out, l, w_router[l], b_router[l])
        dest, pend, be, nxt, n_used = _route(te.reshape(B * S, LANES)[:, :TOP_K], ne)
        xs = _dispatch(h2.reshape(B * S, D), dest, pend, be.shape[0] * EXPERT_TILE)
        outs = _experts(xs, be, nxt, n_used, l, w_up, b_up, w_down, b_down)
        xp = _combine(outs, dest, tg, x1, mod6)

    return _from_perm(_final_norm(xp, g_final))
```

```python
import functools
import math

import numpy as np
import jax
import jax.numpy as jnp
from jax import lax
from jax.experimental import pallas as pl
from jax.experimental.pallas import tpu as pltpu

HEAD_DIM = 64
LANES = 128
BLK = 256
BAND = 128
DILATIONS = (1, 4, 16)
MOBA_TOPK = 3
TOP_K = 4
SWIGLU_LIMIT = 7.0
SWIGLU_ALPHA = 1.702
ROPE_THETA = 10000.0
NORM_EPS = 1e-6
NEG = -1e30
EXPERT_TILE = 256
VMEM_LIMIT = 60000 * 1024

_ARB = pltpu.ARBITRARY


def _cparams(n_axes):
    return pltpu.CompilerParams(dimension_semantics=(_ARB,) * n_axes, vmem_limit_bytes=VMEM_LIMIT)


def _to_perm(x):
    B, S = x.shape[:2]
    rest = x.shape[2:]
    x = x.reshape(B, S // BLK, 16, 4, 4, *rest)
    x = jnp.swapaxes(x, 2, 4)
    return x.reshape(B, S, *rest)


def _from_perm(x):
    B, S = x.shape[:2]
    rest = x.shape[2:]
    x = x.reshape(B, S // BLK, 4, 4, 16, *rest)
    x = jnp.swapaxes(x, 2, 4)
    return x.reshape(B, S, *rest)


def _local_pos():
    p = np.arange(BLK)
    return 16 * (p % 16) + 4 * ((p // 16) % 4) + p // 64


def _band_bias(seq_q, seq_k):
    dist = seq_q[:, None] - seq_k[None, :]
    ok = (dist >= 0) & (dist <= BAND)
    half = seq_k.shape[0] // 2
    exists = np.arange(seq_k.shape[0])[None, :] >= half
    normal = np.where(ok, 0.0, NEG).astype(np.float32)
    first = np.where(ok & exists, 0.0, NEG).astype(np.float32)
    two = np.stack([first, normal])
    return np.concatenate([two, two], axis=1)


def _dilated_biases():
    lp = _local_pos()
    b1 = _band_bias(lp + BLK, np.concatenate([lp, lp + BLK]))
    hi, c, lo = np.meshgrid(np.arange(4), np.arange(4), np.arange(16), indexing="ij")
    seq_k4 = (64 * hi + 4 * lo + c).reshape(-1)
    seq_q4 = seq_k4[:128] + 128
    b4 = _band_bias(seq_q4, seq_k4)
    b16 = _band_bias(np.arange(128) + 128, np.arange(256))
    return b1, b4, b16


def _own_bias():
    lp = _local_pos()
    c = np.where(lp[None, :] <= lp[:, None], 0.0, NEG).astype(np.float32)
    return np.concatenate([c, c, c, c], axis=0)


def _stack_heads(q):
    lane = lax.broadcasted_iota(jnp.int32, q.shape, 1)
    zero = jnp.zeros_like(q)
    return jnp.concatenate([jnp.where(lane < HEAD_DIM, q, zero), jnp.where(lane >= HEAD_DIM, q, zero)], axis=0)


def _scores(qs, kk):
    return lax.dot_general(qs, kk, (((1,), (1,)), ((), ())), preferred_element_type=jnp.float32)


def _tile_lanes(x, k):
    return x if k == 1 else jnp.concatenate([x] * k, axis=1)


def _softmax_step(s, vv, m_old, l_old, acc_old):
    m_cur = jnp.max(s, axis=1, keepdims=True)
    m_new = jnp.maximum(m_old, m_cur)
    alpha = jnp.exp(m_old - m_new)
    p = jnp.exp(s - _tile_lanes(m_new, s.shape[1] // LANES))
    l_new = alpha * l_old + jnp.sum(p, axis=1, keepdims=True)
    pv = jnp.dot(p.astype(vv.dtype), vv, preferred_element_type=jnp.float32)
    return m_new, l_new, alpha * acc_old + pv


def _merge_heads(x):
    n = x.shape[0] // 2
    lane = lax.broadcasted_iota(jnp.int32, (n, LANES), 1)
    return jnp.where(lane < HEAD_DIM, x[:n], x[n:])


def _ada_kernel(c_ref, w_ref, b_ref, o_ref):
    c = c_ref[...]
    sc = (c * (1.0 / (1.0 + jnp.exp(-c)))).astype(jnp.bfloat16)
    o_ref[...] = jnp.dot(sc, w_ref[...].astype(jnp.bfloat16), preferred_element_type=jnp.float32) + b_ref[...]


def _ada_mod(c, w_ada, b_ada):
    depth, d, n = w_ada.shape
    rows = 8
    tn = math.gcd(n, 1024)
    cp = jnp.zeros((rows, d), jnp.float32).at[: c.shape[0]].set(c)
    return pl.pallas_call(
        _ada_kernel,
        grid=(depth, n // tn),
        in_specs=[
            pl.BlockSpec((rows, d), lambda l, j: (0, 0)),
            pl.BlockSpec((None, d, tn), lambda l, j: (l, 0, j)),
            pl.BlockSpec((None, 1, tn), lambda l, j: (l, 0, j)),
        ],
        out_specs=pl.BlockSpec((None, rows, tn), lambda l, j: (l, 0, j)),
        out_shape=jax.ShapeDtypeStruct((depth, rows, n), jnp.float32),
        compiler_params=_cparams(2),
        name="ada_mod",
    )(cp, w_ada, b_ada.reshape(depth, 1, n))


def _rms_mod(x, scale, shift):
    ms = jnp.mean(x * x, axis=-1, keepdims=True)
    return x * lax.rsqrt(ms + NORM_EPS) * (1.0 + scale) + shift


def _cast_rows(src_ref, dst_ref):
    rows_per_step = math.gcd(src_ref.shape[0], 256)
    n = src_ref.shape[0] // rows_per_step

    def body(i, carry):
        r = pl.multiple_of(i * rows_per_step, rows_per_step)
        dst_ref[pl.ds(r, rows_per_step), :] = src_ref[pl.ds(r, rows_per_step), :].astype(dst_ref.dtype)
        return carry

    lax.fori_loop(0, n, body, 0)


def _qkv_kernel(x_ref, shift_ref, scale_ref, w_ref, cos_ref, sin_ref, o_ref, wbf_ref):
    b = pl.program_id(1)
    i = pl.program_id(2)

    @pl.when((b == 0) & (i == 0))
    def _():
        _cast_rows(w_ref, wbf_ref)

    h = _rms_mod(x_ref[...], scale_ref[...], shift_ref[...]).astype(jnp.bfloat16)
    acc = jnp.dot(h, wbf_ref[...], preferred_element_type=jnp.float32)
    cos = cos_ref[...]
    sin = sin_ref[...]
    lane = lax.broadcasted_iota(jnp.int32, cos.shape, 1)
    first_half = (lane % HEAD_DIM) < (HEAD_DIM // 2)
    for c in range(o_ref.shape[0]):
        a = acc[:, c * LANES:(c + 1) * LANES]
        partner = jnp.where(first_half, pltpu.roll(a, LANES - HEAD_DIM // 2, 1), pltpu.roll(a, HEAD_DIM // 2, 1))
        o_ref[c] = (a * cos + partner * sin).astype(o_ref.dtype)


def _qkv_proj(x, mod6, w_in, layer, cos_t, sin_t, tm=512):
    B, S, D = x.shape
    n3 = w_in.shape[2]
    width = n3 // 3
    pairs = width // LANES
    return pl.pallas_call(
        _qkv_kernel,
        grid=(3, B, S // tm),
        in_specs=[
            pl.BlockSpec((None, tm, D), lambda j, b, i: (b, i, 0)),
            pl.BlockSpec((None, None, 1, D), lambda j, b, i: (b, 0, 0, 0)),
            pl.BlockSpec((None, None, 1, D), lambda j, b, i: (b, 1, 0, 0)),
            pl.BlockSpec((None, D, width), lambda j, b, i: (layer, 0, j), pipeline_mode=pl.Buffered(1)),
            pl.BlockSpec((None, None, tm, LANES), lambda j, b, i: (j, b, i, 0)),
            pl.BlockSpec((None, None, tm, LANES), lambda j, b, i: (j, b, i, 0)),
        ],
        out_specs=pl.BlockSpec((None, pairs, tm, LANES), lambda j, b, i: (b, j, i, 0)),
        out_shape=jax.ShapeDtypeStruct((B, 3 * pairs, S, LANES), jnp.bfloat16),
        scratch_shapes=[pltpu.VMEM((D, width), jnp.bfloat16)],
        compiler_params=_cparams(3),
        name="qkv_proj",
    )(x, mod6, mod6, w_in, cos_t, sin_t)


def _dilated_kernel(q_ref, k_ref, v_ref, b1_ref, b4_ref, b16_ref, o_ref, m_ref, l_ref, acc_ref):
    nb = q_ref.shape[0]

    def rows(x):
        return x.reshape(-1, LANES)

    def update(items):
        old = [(acc_ref[idx], m_ref[(0,) + idx], m_ref[(1,) + idx], l_ref[(0,) + idx], l_ref[(1,) + idx])
               for (_, _, _, _, idx) in items]
        new = []
        for (q, kk, vv, bias, idx), (a_raw, m0, m1, l0, l1) in zip(items, old):
            s = _scores(_stack_heads(q), kk) + bias
            m_old = jnp.concatenate([rows(m0), rows(m1)], axis=0)
            l_old = jnp.concatenate([rows(l0), rows(l1)], axis=0)
            a_old = jnp.concatenate([rows(a_raw), rows(a_raw)], axis=0)
            new.append(_softmax_step(s, vv, m_old, l_old, a_old) + (a_raw.shape, q.shape[0]))
        for (_, _, _, _, idx), (m_new, l_new, a_new, shp, n) in zip(items, new):
            m_ref[(0,) + idx] = m_new[:n].reshape(shp)
            m_ref[(1,) + idx] = m_new[n:].reshape(shp)
            l_ref[(0,) + idx] = l_new[:n].reshape(shp)
            l_ref[(1,) + idx] = l_new[n:].reshape(shp)
            acc_ref[idx] = _merge_heads(a_new).reshape(shp)

    def init(n, carry):
        m_ref[0, n] = jnp.full(m_ref.shape[2:], NEG, jnp.float32)
        m_ref[1, n] = jnp.full(m_ref.shape[2:], NEG, jnp.float32)
        l_ref[0, n] = jnp.zeros(l_ref.shape[2:], jnp.float32)
        l_ref[1, n] = jnp.zeros(l_ref.shape[2:], jnp.float32)
        acc_ref[n] = jnp.zeros(acc_ref.shape[1:], jnp.float32)
        return carry

    lax.fori_loop(0, nb, init, 0)

    def run(n_blocks, group, make_item):
        def step(t, carry):
            update([make_item(group * t + g) for g in range(group)])
            return carry

        lax.fori_loop(0, n_blocks // group, step, 0)

    def d1(n):
        prev = jnp.maximum(n - 1, 0)
        kk = jnp.concatenate([rows(k_ref[prev]), rows(k_ref[n])], axis=0)
        vv = jnp.concatenate([rows(v_ref[prev]), rows(v_ref[n])], axis=0)
        return rows(q_ref[n]), kk, vv, b1_ref[jnp.minimum(n, 1)], (n,)

    run(nb, 4, d1)

    per4 = nb // 2

    def d4(t):
        r4 = t // per4
        a = t % per4
        cur = pl.ds(2 * a, 2)
        prev = pl.ds(jnp.maximum(2 * a - 2, 0), 2)
        grp = pl.ds(4 * r4, 4)
        kk = jnp.concatenate([rows(k_ref[prev, grp]), rows(k_ref[cur, grp])], axis=0)
        vv = jnp.concatenate([rows(v_ref[prev, grp]), rows(v_ref[cur, grp])], axis=0)
        return rows(q_ref[cur, grp]), kk, vv, b4_ref[jnp.minimum(a, 1)], (cur, grp)

    run(4 * per4, 8, d4)

    per16 = nb // 8

    def d16(t):
        rr = t // per16
        b = t % per16
        cur = pl.ds(8 * b, 8)
        prev = pl.ds(jnp.maximum(8 * b - 8, 0), 8)
        kk = jnp.concatenate([rows(k_ref[prev, rr]), rows(k_ref[cur, rr])], axis=0)
        vv = jnp.concatenate([rows(v_ref[prev, rr]), rows(v_ref[cur, rr])], axis=0)
        return rows(q_ref[cur, rr]), kk, vv, b16_ref[jnp.minimum(b, 1)], (cur, rr)

    run(16 * per16, 8, d16)

    def fin(n, carry):
        lane = lax.broadcasted_iota(jnp.int32, acc_ref.shape[1:], 2)
        den = jnp.where(lane < HEAD_DIM, l_ref[0, n], l_ref[1, n])
        o_ref[n] = (acc_ref[n] / den).astype(o_ref.dtype)
        return carry

    lax.fori_loop(0, nb, fin, 0)


def _seq_view(a):
    B, P, S, L = a.shape
    return a.reshape(B, P, S // BLK, 16, 16, L)


def _dilated_attention(qkv, biases):
    B, P3, S, _ = qkv.shape
    P = P3 // 3
    pa = P // 2
    nb = S // BLK
    assert S % (16 * BAND) == 0, "sequence must be a whole number of dilation-16 band blocks"
    v6 = _seq_view(qkv)
    blk = (None, None, nb, 16, 16, LANES)
    b1, b4, b16 = biases
    out = pl.pallas_call(
        _dilated_kernel,
        grid=(B, pa),
        in_specs=[
            pl.BlockSpec(blk, lambda b, p: (b, p, 0, 0, 0, 0)),
            pl.BlockSpec(blk, lambda b, p: (b, P + p, 0, 0, 0, 0)),
            pl.BlockSpec(blk, lambda b, p: (b, 2 * P + p, 0, 0, 0, 0)),
            pl.BlockSpec(b1.shape, lambda b, p: (0, 0, 0)),
            pl.BlockSpec(b4.shape, lambda b, p: (0, 0, 0)),
            pl.BlockSpec(b16.shape, lambda b, p: (0, 0, 0)),
        ],
        out_specs=pl.BlockSpec(blk, lambda b, p: (b, p, 0, 0, 0, 0)),
        out_shape=jax.ShapeDtypeStruct((B, pa, nb, 16, 16, LANES), jnp.bfloat16),
        scratch_shapes=[
            pltpu.VMEM((2, nb, 16, 16, LANES), jnp.float32),
            pltpu.VMEM((2, nb, 16, 16, LANES), jnp.float32),
            pltpu.VMEM((nb, 16, 16, LANES), jnp.float32),
        ],
        compiler_params=_cparams(2),
        name="dilated_attn",
    )(v6, v6, v6, b1, b4, b16)
    return out.reshape(B, pa, S, LANES)


def _moba_kernel(q_ref, k_ref, v_ref, own_ref, hot_ref, o_ref, kmean_ref, qa_ref, m_ref, l_ref, acc_ref):
    m = pl.program_id(2)
    nb = k_ref.shape[0]
    nbp = -(-nb // 8) * 8

    def rows(x):
        return x.reshape(-1, LANES)

    @pl.when(m == 0)
    def _():
        kmean_ref[...] = jnp.zeros(kmean_ref.shape, jnp.float32)
        for j in range(nb):
            kmean_ref[pl.ds(j, 1), :] = jnp.mean(rows(k_ref[j]).astype(jnp.float32), axis=0, keepdims=True)

    qs = jnp.concatenate([_stack_heads(rows(q_ref[0])), _stack_heads(rows(q_ref[1]))], axis=0)
    nq = qs.shape[0]
    half = nq // 2

    gate = _scores(kmean_ref[...].astype(jnp.bfloat16), qs)[:nbp]
    jidx = lax.broadcasted_iota(jnp.int32, gate.shape, 0)
    own = 2 * m + lax.broadcasted_iota(jnp.int32, gate.shape, 1) // half
    g = jnp.where(jidx < own, gate, -jnp.inf)
    rank = jnp.zeros(gate.shape, jnp.int32)
    for jp in range(nb):
        other = g[jp:jp + 1, :]
        beats = (other > g) | ((other == g) & (jp < jidx))
        rank = rank + beats.astype(jnp.int32)
    keep = (rank < MOBA_TOPK) & (jidx < own)
    pen_t = jnp.where(keep, 0.0, NEG).astype(jnp.float32)
    pen_t = jnp.concatenate([pen_t, jnp.full((LANES - nbp, nq), NEG, jnp.float32)], axis=0)
    qa_ref[...] = jnp.concatenate([qs, pen_t.T.astype(qs.dtype)], axis=1)

    ka, kb = rows(k_ref[2 * m]), rows(k_ref[2 * m + 1])
    s = jnp.concatenate([_scores(qs[:half], ka), _scores(qs[half:], kb)], axis=0) + own_ref[...]
    m0 = jnp.max(s, axis=1, keepdims=True)
    p = jnp.exp(s - m0)
    m_ref[...] = jnp.broadcast_to(m0, m_ref.shape)
    l_ref[...] = jnp.broadcast_to(jnp.sum(p, axis=1, keepdims=True), l_ref.shape)
    pb = p.astype(jnp.bfloat16)
    acc_ref[...] = jnp.concatenate(
        [jnp.dot(pb[:half], rows(v_ref[2 * m]), preferred_element_type=jnp.float32),
         jnp.dot(pb[half:], rows(v_ref[2 * m + 1]), preferred_element_type=jnp.float32)], axis=0)

    hi = pl.ds(half, half)
    sj = _scores(qa_ref[hi, :], jnp.concatenate([ka, hot_ref[2 * m]], axis=1))
    m_new, l_new, a_new = _softmax_step(sj, rows(v_ref[2 * m]), m_ref[hi, :], l_ref[hi, :], acc_ref[hi, :])
    m_ref[hi, :] = m_new
    l_ref[hi, :] = l_new
    acc_ref[hi, :] = a_new

    def past(t, carry):
        kk = jnp.concatenate([jnp.concatenate([rows(k_ref[2 * t]), hot_ref[2 * t]], axis=1),
                              jnp.concatenate([rows(k_ref[2 * t + 1]), hot_ref[2 * t + 1]], axis=1)], axis=0)
        vv = jnp.concatenate([rows(v_ref[2 * t]), rows(v_ref[2 * t + 1])], axis=0)
        m_new, l_new, a_new = _softmax_step(_scores(qa_ref[...], kk), vv, m_ref[...], l_ref[...], acc_ref[...])
        m_ref[...] = m_new
        l_ref[...] = l_new
        acc_ref[...] = a_new
        return carry

    lax.fori_loop(0, m, past, 0)

    out = acc_ref[...] / l_ref[...]
    o_ref[0] = _merge_heads(out[:half]).reshape(o_ref.shape[1:]).astype(o_ref.dtype)
    o_ref[1] = _merge_heads(out[half:]).reshape(o_ref.shape[1:]).astype(o_ref.dtype)


def _block_one_hot(nb):
    hot = np.zeros((nb, BLK, LANES), np.float32)
    hot[np.arange(nb), :, np.arange(nb)] = 1.0
    return jnp.asarray(hot, jnp.bfloat16)


def _moba_attention(qkv, own_bias):
    B, P3, S, _ = qkv.shape
    P = P3 // 3
    pa = P // 2
    nb = S // BLK
    assert nb <= LANES
    v6 = _seq_view(qkv)
    hot = _block_one_hot(nb)
    full = (None, None, nb, 16, 16, LANES)
    tile = (None, None, 2, 16, 16, LANES)
    nq = 4 * BLK
    out = pl.pallas_call(
        _moba_kernel,
        grid=(B, pa, nb // 2),
        in_specs=[
            pl.BlockSpec(tile, lambda b, p, n: (b, pa + p, n, 0, 0, 0)),
            pl.BlockSpec(full, lambda b, p, n: (b, P + pa + p, 0, 0, 0, 0)),
            pl.BlockSpec(full, lambda b, p, n: (b, 2 * P + pa + p, 0, 0, 0, 0)),
            pl.BlockSpec(own_bias.shape, lambda b, p, n: (0, 0)),
            pl.BlockSpec(hot.shape, lambda b, p, n: (0, 0, 0)),
        ],
        out_specs=pl.BlockSpec(tile, lambda b, p, n: (b, p, n, 0, 0, 0)),
        out_shape=jax.ShapeDtypeStruct((B, pa, nb, 16, 16, LANES), jnp.bfloat16),
        scratch_shapes=[
            pltpu.VMEM((LANES, LANES), jnp.float32),
            pltpu.VMEM((nq, 2 * LANES), jnp.bfloat16),
            pltpu.VMEM((nq, LANES), jnp.float32),
            pltpu.VMEM((nq, LANES), jnp.float32),
            pltpu.VMEM((nq, LANES), jnp.float32),
        ],
        compiler_params=_cparams(3),
        name="moba_attn",
    )(v6, v6, v6, own_bias, hot)
    return out.reshape(B, pa, S, LANES)


def _post_kernel(oa_ref, ob_ref, x_ref, gate_a_ref, shift_f_ref, scale_f_ref, beta_ref, w_ref, wr_ref, br_ref,
                 x1_ref, h2_ref, te_ref, tg_ref, wbf_ref, y_ref, whi_ref, wlo_ref):
    b = pl.program_id(0)
    i = pl.program_id(1)

    @pl.when((b == 0) & (i == 0))
    def _():
        _cast_rows(w_ref, wbf_ref)
        wr = wr_ref[...]
        hi = wr.astype(jnp.bfloat16)
        whi_ref[...] = hi
        wlo_ref[...] = (wr - hi.astype(jnp.float32)).astype(jnp.bfloat16)

    half = oa_ref.shape[0] * LANES
    for off, o_ref in ((0, oa_ref), (half, ob_ref)):
        ss = None
        for c in range(o_ref.shape[0]):
            o = o_ref[c].astype(jnp.float32)
            part = jnp.sum(o * o, axis=-1, keepdims=True)
            ss = part if ss is None else ss + part
        inv = lax.rsqrt(ss / half + NORM_EPS)
        for c in range(o_ref.shape[0]):
            lo = off + c * LANES
            y_ref[:, lo:lo + LANES] = (o_ref[c].astype(jnp.float32) * inv * beta_ref[:, lo:lo + LANES]).astype(y_ref.dtype)

    mixed = jnp.dot(y_ref[...], wbf_ref[...], preferred_element_type=jnp.float32)
    x1 = x_ref[...] + gate_a_ref[...] * mixed
    x1_ref[...] = x1
    h2 = _rms_mod(x1, scale_f_ref[...], shift_f_ref[...])
    h2_ref[...] = h2

    h_hi = h2.astype(jnp.bfloat16)
    h_lo = (h2 - h_hi.astype(jnp.float32)).astype(jnp.bfloat16)
    logits = (jnp.dot(h_hi, whi_ref[...], preferred_element_type=jnp.float32)
              + jnp.dot(h_hi, wlo_ref[...], preferred_element_type=jnp.float32)
              + jnp.dot(h_lo, whi_ref[...], preferred_element_type=jnp.float32)) + br_ref[...]
    lane = lax.broadcasted_iota(jnp.int32, logits.shape, 1).astype(jnp.float32)
    te = jnp.zeros(logits.shape, jnp.float32)
    tg = jnp.zeros(logits.shape, jnp.float32)
    work = logits
    top0 = None
    den = None
    for kk in range(TOP_K):
        mv = jnp.max(work, axis=1, keepdims=True)
        idx = jnp.min(jnp.where(work == mv, lane, float(LANES)), axis=1, keepdims=True)
        work = jnp.where(lane == idx, -jnp.inf, work)
        if kk == 0:
            top0 = mv
        e = jnp.exp(mv - top0)
        den = e if den is None else den + e
        te = jnp.where(lane == kk, idx, te)
        tg = jnp.where(lane == kk, e, tg)
    te_ref[...] = te.astype(jnp.int32)
    tg_ref[...] = tg / den


def _post_attention(oa, ob, x, mod6, beta, w_out, layer, w_router, b_router, tm=256):
    B, S, D = x.shape
    pa = oa.shape[1]
    ne = w_router.shape[1]
    assert TOP_K <= ne <= LANES
    w_router = jnp.pad(w_router, ((0, 0), (0, LANES - ne)))
    b_router = jnp.pad(b_router, (0, LANES - ne), constant_values=-jnp.inf)
    ne = LANES
    row = lambda k: pl.BlockSpec((None, None, 1, D), lambda b, i, k=k: (b, k, 0, 0))
    tok = pl.BlockSpec((None, tm, D), lambda b, i: (b, i, 0))
    small = pl.BlockSpec((None, tm, LANES), lambda b, i: (b, i, 0))
    return pl.pallas_call(
        _post_kernel,
        grid=(B, S // tm),
        in_specs=[
            pl.BlockSpec((None, pa, tm, LANES), lambda b, i: (b, 0, i, 0)),
            pl.BlockSpec((None, pa, tm, LANES), lambda b, i: (b, 0, i, 0)),
            tok, row(2), row(3), row(4),
            pl.BlockSpec((1, D), lambda b, i: (0, 0)),
            pl.BlockSpec((None,) + w_out.shape[1:], lambda b, i: (layer, 0, 0), pipeline_mode=pl.Buffered(1)),
            pl.BlockSpec(w_router.shape, lambda b, i: (0, 0)),
            pl.BlockSpec((1, ne), lambda b, i: (0, 0)),
        ],
        out_specs=[tok, tok, small, small],
        out_shape=[
            jax.ShapeDtypeStruct((B, S, D), jnp.float32),
            jax.ShapeDtypeStruct((B, S, D), jnp.float32),
            jax.ShapeDtypeStruct((B, S, LANES), jnp.int32),
            jax.ShapeDtypeStruct((B, S, LANES), jnp.float32),
        ],
        scratch_shapes=[pltpu.VMEM(w_out.shape[1:], jnp.bfloat16), pltpu.VMEM((tm, D), jnp.bfloat16),
                        pltpu.VMEM(w_router.shape, jnp.bfloat16), pltpu.VMEM(w_router.shape, jnp.bfloat16)],
        compiler_params=_cparams(2),
        name="post_attn_router",
    )(oa, ob, x, mod6, mod6, mod6, beta, w_out, w_router, b_router.reshape(1, ne))


def _route(top_e, n_experts):
    T = top_e.shape[0]
    A = T * TOP_K
    onehot = (top_e[:, :, None] == jnp.arange(n_experts, dtype=jnp.int32)[None, None, :]).astype(jnp.int32)
    tot = jnp.sum(onehot, axis=1)
    incl = jnp.cumsum(tot, axis=0)
    counts = incl[-1]
    padded = ((counts + EXPERT_TILE - 1) // EXPERT_TILE) * EXPERT_TILE
    pend = jnp.cumsum(padded)
    pstart = pend - padded
    dest = jnp.sum(onehot * (incl - tot + pstart[None, :])[:, None, :], axis=2).astype(jnp.int32)
    n_slots = A + n_experts * EXPERT_TILE
    n_blocks = n_slots // EXPERT_TILE
    n_used = (pend[-1] // EXPERT_TILE).astype(jnp.int32)
    blk = jnp.arange(n_blocks, dtype=jnp.int32)
    be = jnp.sum((pend[None, :] <= (blk * EXPERT_TILE)[:, None]).astype(jnp.int32), axis=1)
    be = jnp.minimum(be, n_experts - 1)
    last = jnp.sum(jnp.where(blk == n_used - 1, be, 0))
    be = jnp.where(blk < n_used, be, last)
    group_end = jnp.sum(jnp.where(be[:, None] == jnp.arange(n_experts)[None, :], pend[None, :], 0), axis=1)
    after = group_end // EXPERT_TILE
    be_after = jnp.sum(jnp.where(blk[None, :] == after[:, None], be[None, :], 0), axis=1)
    nxt = jnp.where(after < n_used, be_after, -1).astype(jnp.int32)
    return dest, pend.astype(jnp.int32), be, nxt, n_used.reshape(1)


def _tile_major(dest, tm):
    T = dest.shape[0]
    return dest.reshape(T // tm, tm, TOP_K).transpose(0, 2, 1).reshape(T // tm, 1, TOP_K * tm)


def _issue_pairs(n, make_copy, wait=False):
    if not wait:
        for r in range(n):
            make_copy(r).start(priority=r % 2)
        return

    def body(u, c):
        make_copy(2 * u).wait()
        make_copy(2 * u + 1).wait()
        return c

    lax.fori_loop(0, n // 2, body, 0, unroll=4)


def _dispatch_kernel(pend_ref, dest_ref, h_ref, xs_hbm, zbuf, sem):
    i = pl.program_id(0)
    tm = dest_ref.shape[1] // TOP_K
    tile = zbuf.shape[0]
    ne = pend_ref.shape[0]

    @pl.when(i == 0)
    def _():
        zbuf[...] = jnp.zeros(zbuf.shape, zbuf.dtype)

        def fill(e):
            end = pend_ref[e]
            start = jnp.where(e == 0, 0, pend_ref[jnp.maximum(e - 1, 0)])
            last = pl.multiple_of(jnp.maximum(end - tile, 0), tile)
            return end > start, pltpu.make_async_copy(zbuf, xs_hbm.at[pl.ds(last, tile)], sem)

        def start(e, c):
            ok, cp = fill(e)

            @pl.when(ok)
            def _():
                cp.start()
            return c

        def wait(e, c):
            ok, cp = fill(e)

            @pl.when(ok)
            def _():
                cp.wait()
            return c

        lax.fori_loop(0, ne, start, 0)
        lax.fori_loop(0, ne, wait, 0)

        def tail(b):
            off = pl.multiple_of(b * tile, tile)
            return off >= pend_ref[ne - 1], pltpu.make_async_copy(zbuf, xs_hbm.at[pl.ds(off, tile)], sem)

        def tail_start(b, c):
            ok, cp = tail(b)

            @pl.when(ok)
            def _():
                cp.start()
            return c

        def tail_wait(b, c):
            ok, cp = tail(b)

            @pl.when(ok)
            def _():
                cp.wait()
            return c

        n_tiles = xs_hbm.shape[0] // tile
        lax.fori_loop(0, n_tiles, tail_start, 0)
        lax.fori_loop(0, n_tiles, tail_wait, 0)

    def row_copy(r):
        return pltpu.make_async_copy(h_ref.at[pl.ds(r % tm, 1)], xs_hbm.at[pl.ds(dest_ref[0, r], 1)], sem)

    _issue_pairs(TOP_K * tm, row_copy)
    _issue_pairs(TOP_K * tm, row_copy, wait=True)


def _dispatch(h2, dest, pend, n_slots, tm=128):
    T, D = h2.shape
    grid_spec = pltpu.PrefetchScalarGridSpec(
        num_scalar_prefetch=1,
        grid=(T // tm,),
        in_specs=[
            pl.BlockSpec((None, 1, TOP_K * tm), lambda i, pe: (i, 0, 0), memory_space=pltpu.SMEM),
            pl.BlockSpec((tm, D), lambda i, pe: (i, 0)),
        ],
        out_specs=pl.BlockSpec(memory_space=pl.ANY),
        scratch_shapes=[pltpu.VMEM((EXPERT_TILE, D), h2.dtype), pltpu.SemaphoreType.DMA(())],
    )
    return pl.pallas_call(
        _dispatch_kernel,
        grid_spec=grid_spec,
        out_shape=jax.ShapeDtypeStruct((n_slots, D), h2.dtype),
        compiler_params=_cparams(1),
        name="moe_dispatch",
    )(pend, _tile_major(dest, tm), h2)


def _expert_kernel(be_ref, nxt_ref, nused_ref, x_ref, wu_hbm, bu_ref, wd_hbm, bd_ref, o_ref,
                   wu32, wd32, wub, wdb, sems, *, layer):
    i = pl.program_id(0)
    d_ff = wd32.shape[0]

    def fetch(e):
        return (pltpu.make_async_copy(wu_hbm.at[layer, e], wu32, sems.at[0]),
                pltpu.make_async_copy(wd_hbm.at[layer, e], wd32, sems.at[1]))

    @pl.when(i == 0)
    def _():
        for cp in fetch(be_ref[0]):
            cp.start()

    @pl.when(i < nused_ref[0])
    def _():
        @pl.when((i == 0) | (be_ref[i] != be_ref[jnp.maximum(i - 1, 0)]))
        def _():
            for cp in fetch(be_ref[i]):
                cp.wait()
            _cast_rows(wu32, wub)
            _cast_rows(wd32, wdb)

            @pl.when(nxt_ref[i] >= 0)
            def _():
                for cp in fetch(nxt_ref[i]):
                    cp.start()

        x = x_ref[...].astype(jnp.bfloat16)
        u = jnp.dot(x, wub[...], preferred_element_type=jnp.float32) + bu_ref[...]
        glu = jnp.minimum(u[:, :d_ff], SWIGLU_LIMIT)
        lin = jnp.clip(u[:, d_ff:], -SWIGLU_LIMIT, SWIGLU_LIMIT)
        act = glu * (1.0 / (1.0 + jnp.exp(-SWIGLU_ALPHA * glu))) * (lin + 1.0)
        o_ref[...] = jnp.dot(act.astype(jnp.bfloat16), wdb[...], preferred_element_type=jnp.float32) + bd_ref[...]

    @pl.when(i >= nused_ref[0])
    def _():
        o_ref[...] = jnp.zeros(o_ref.shape, o_ref.dtype)


def _experts(xs, be, nxt, n_used, layer, w_up, b_up, w_down, b_down):
    n_slots, D = xs.shape
    depth, ne, _, f2 = w_up.shape
    d_ff = w_down.shape[2]
    tm = EXPERT_TILE
    n_blocks = n_slots // tm
    grid_spec = pltpu.PrefetchScalarGridSpec(
        num_scalar_prefetch=3,
        grid=(n_blocks,),
        in_specs=[
            pl.BlockSpec((tm, D), lambda i, be, nx, nu: (jnp.minimum(i, nu[0] - 1), 0)),
            pl.BlockSpec(memory_space=pl.ANY),
            pl.BlockSpec((None, None, 1, f2), lambda i, be, nx, nu: (layer, be[i], 0, 0)),
            pl.BlockSpec(memory_space=pl.ANY),
            pl.BlockSpec((None, None, 1, D), lambda i, be, nx, nu: (layer, be[i], 0, 0)),
        ],
        out_specs=pl.BlockSpec((tm, D), lambda i, be, nx, nu: (i, 0)),
        scratch_shapes=[
            pltpu.VMEM((D, f2), jnp.float32),
            pltpu.VMEM((d_ff, D), jnp.float32),
            pltpu.VMEM((D, f2), jnp.bfloat16),
            pltpu.VMEM((d_ff, D), jnp.bfloat16),
            pltpu.SemaphoreType.DMA((2,)),
        ],
    )
    return pl.pallas_call(
        functools.partial(_expert_kernel, layer=layer),
        grid_spec=grid_spec,
        out_shape=jax.ShapeDtypeStruct((n_slots, D), jnp.float32),
        compiler_params=_cparams(1),
        name="experts",
    )(be, nxt, n_used, xs, w_up, b_up.reshape(depth, ne, 1, f2), w_down, b_down.reshape(depth, ne, 1, D))


def _combine_kernel(dest_ref, y_hbm, x_ref, gate_ref, tg_ref, gfin_ref, o_ref, ybuf, sem, *, final):
    tm = x_ref.shape[0]

    def row_copy(r):
        return pltpu.make_async_copy(y_hbm.at[pl.ds(dest_ref[0, r], 1)], ybuf.at[pl.ds(r, 1)], sem)

    _issue_pairs(TOP_K * tm, row_copy)
    _issue_pairs(TOP_K * tm, row_copy, wait=True)
    tg = tg_ref[...]
    y = None
    for k in range(TOP_K):
        part = tg[:, k:k + 1] * ybuf[k * tm:(k + 1) * tm, :]
        y = part if y is None else y + part
    x2 = x_ref[...] + gate_ref[...] * y
    if final:
        ms = jnp.mean(x2 * x2, axis=-1, keepdims=True)
        x2 = x2 * lax.rsqrt(ms + NORM_EPS) * gfin_ref[...]
    o_ref[...] = x2


def _combine(outs, dest, tg, x1, mod6, g_final, final, tm=128):
    B, S, D = x1.shape
    nt = S // tm
    tok = pl.BlockSpec((None, tm, D), lambda b, i: (b, i, 0))
    return pl.pallas_call(
        functools.partial(_combine_kernel, final=final),
        grid=(B, nt),
        in_specs=[
            pl.BlockSpec((None, 1, TOP_K * tm), lambda b, i: (b * nt + i, 0, 0), memory_space=pltpu.SMEM),
            pl.BlockSpec(memory_space=pl.ANY),
            tok,
            pl.BlockSpec((None, None, 1, D), lambda b, i: (b, 5, 0, 0)),
            pl.BlockSpec((None, tm, LANES), lambda b, i: (b, i, 0)),
            pl.BlockSpec((1, D), lambda b, i: (0, 0)),
        ],
        out_specs=tok,
        out_shape=jax.ShapeDtypeStruct((B, S, D), jnp.float32),
        scratch_shapes=[pltpu.VMEM((TOP_K * tm, D), jnp.float32), pltpu.SemaphoreType.DMA(())],
        compiler_params=_cparams(2),
        name="moe_combine",
    )(_tile_major(dest, tm), outs, x1, mod6, tg, g_final.reshape(1, D))


def _rotary_tables(positions, scale_q):
    inv_freq = 1.0 / (ROPE_THETA ** (jnp.arange(0, HEAD_DIM, 2, dtype=jnp.float32) / HEAD_DIM))
    ang = positions[..., None].astype(jnp.float32) * inv_freq
    cos, sin = jnp.cos(ang), jnp.sin(ang)
    reps = LANES // HEAD_DIM
    cos_l = jnp.tile(jnp.concatenate([cos, cos], axis=-1), (1, 1, reps))
    sin_l = jnp.tile(jnp.concatenate([-sin, sin], axis=-1), (1, 1, reps))
    cos_t = jnp.stack([cos_l * scale_q, cos_l, jnp.ones_like(cos_l)])
    sin_t = jnp.stack([sin_l * scale_q, sin_l, jnp.zeros_like(sin_l)])
    return cos_t, sin_t


def kernel(x, c, positions, w_ada, b_ada, w_in, beta_a, beta_b, w_out, w_router, b_router, w_up, b_up, w_down, b_down, g_final):
    B, S, D = x.shape
    depth = w_ada.shape[0]
    ne = w_router.shape[-1]
    biases = tuple(jnp.asarray(b) for b in _dilated_biases())
    own_bias = jnp.asarray(_own_bias())

    xp = _to_perm(x)
    pos_p = _to_perm(positions)
    cos_t, sin_t = _rotary_tables(pos_p, HEAD_DIM ** -0.5)
    mod = _ada_mod(c, w_ada, b_ada)

    for l in range(depth):
        mod6 = mod[l, :B].reshape(B, 6, 1, D)
        qkv = _qkv_proj(xp, mod6, w_in, l, cos_t, sin_t)
        oa = _dilated_attention(qkv, biases)
        ob = _moba_attention(qkv, own_bias)
        beta = jnp.concatenate([beta_a[l], beta_b[l]]).reshape(1, D)
        x1, h2, te, tg = _post_attention(oa, ob, xp, mod6, beta, w_out, l, w_router[l], b_router[l])
        dest, pend, be, nxt, n_used = _route(te.reshape(B * S, LANES)[:, :TOP_K], ne)
        xs = _dispatch(h2.reshape(B * S, D), dest, pend, be.shape[0] * EXPERT_TILE)
        outs = _experts(xs, be, nxt, n_used, l, w_up, b_up, w_down, b_down)
        xp = _combine(outs, dest, tg, x1, mod6, g_final, final=(l == depth - 1))

    return _from_perm(xp)
```

```python
import functools
import math

import numpy as np
import jax
import jax.numpy as jnp
from jax import lax
from jax.experimental import pallas as pl
from jax.experimental.pallas import tpu as pltpu

HEAD_DIM = 64
LANES = 128
BLK = 256
BAND = 128
DILATIONS = (1, 4, 16)
MOBA_TOPK = 3
TOP_K = 4
SWIGLU_LIMIT = 7.0
SWIGLU_ALPHA = 1.702
ROPE_THETA = 10000.0
NORM_EPS = 1e-6
NEG = -1e30
EXPERT_TILE = 256
VMEM_LIMIT = 60000 * 1024

_ARB = pltpu.ARBITRARY


def _cparams(n_axes):
    return pltpu.CompilerParams(dimension_semantics=(_ARB,) * n_axes, vmem_limit_bytes=VMEM_LIMIT)


def _to_perm(x):
    B, S = x.shape[:2]
    rest = x.shape[2:]
    x = x.reshape(B, S // BLK, 16, 4, 4, *rest)
    x = jnp.swapaxes(x, 2, 4)
    return x.reshape(B, S, *rest)


def _from_perm(x):
    B, S = x.shape[:2]
    rest = x.shape[2:]
    x = x.reshape(B, S // BLK, 4, 4, 16, *rest)
    x = jnp.swapaxes(x, 2, 4)
    return x.reshape(B, S, *rest)


def _local_pos():
    p = np.arange(BLK)
    return 16 * (p % 16) + 4 * ((p // 16) % 4) + p // 64


def _band_bias(seq_q, seq_k):
    dist = seq_q[:, None] - seq_k[None, :]
    ok = (dist >= 0) & (dist <= BAND)
    half = seq_k.shape[0] // 2
    exists = np.arange(seq_k.shape[0])[None, :] >= half
    normal = np.where(ok, 0.0, NEG).astype(np.float32)
    first = np.where(ok & exists, 0.0, NEG).astype(np.float32)
    two = np.stack([first, normal])
    return np.concatenate([two, two], axis=1)


def _dilated_biases():
    lp = _local_pos()
    b1 = _band_bias(lp + BLK, np.concatenate([lp, lp + BLK]))
    hi, c, lo = np.meshgrid(np.arange(4), np.arange(4), np.arange(16), indexing="ij")
    seq_k4 = (64 * hi + 4 * lo + c).reshape(-1)
    seq_q4 = seq_k4[:128] + 128
    b4 = _band_bias(seq_q4, seq_k4)
    b16 = _band_bias(np.arange(128) + 128, np.arange(256))
    return b1, b4, b16


def _own_bias():
    lp = _local_pos()
    c = np.where(lp[None, :] <= lp[:, None], 0.0, NEG).astype(np.float32)
    return np.concatenate([c, c, c, c], axis=0)


def _stack_heads(q):
    lane = lax.broadcasted_iota(jnp.int32, q.shape, 1)
    zero = jnp.zeros_like(q)
    return jnp.concatenate([jnp.where(lane < HEAD_DIM, q, zero), jnp.where(lane >= HEAD_DIM, q, zero)], axis=0)


def _scores(qs, kk):
    return lax.dot_general(qs, kk, (((1,), (1,)), ((), ())), preferred_element_type=jnp.float32)


def _tile_lanes(x, k):
    return x if k == 1 else jnp.concatenate([x] * k, axis=1)


def _softmax_step(s, vv, m_old, l_old, acc_old):
    m_cur = jnp.max(s, axis=1, keepdims=True)
    m_new = jnp.maximum(m_old, m_cur)
    alpha = jnp.exp(m_old - m_new)
    p = jnp.exp(s - _tile_lanes(m_new, s.shape[1] // LANES))
    l_new = alpha * l_old + jnp.sum(p, axis=1, keepdims=True)
    pv = jnp.dot(p.astype(vv.dtype), vv, preferred_element_type=jnp.float32)
    return m_new, l_new, alpha * acc_old + pv


def _merge_heads(x):
    n = x.shape[0] // 2
    lane = lax.broadcasted_iota(jnp.int32, (n, LANES), 1)
    return jnp.where(lane < HEAD_DIM, x[:n], x[n:])


def _ada_kernel(c_ref, w_ref, b_ref, o_ref):
    c = c_ref[...]
    sc = (c * (1.0 / (1.0 + jnp.exp(-c)))).astype(jnp.bfloat16)
    o_ref[...] = jnp.dot(sc, w_ref[...].astype(jnp.bfloat16), preferred_element_type=jnp.float32) + b_ref[...]


def _ada_mod(c, w_ada, b_ada):
    depth, d, n = w_ada.shape
    rows = 8
    tn = math.gcd(n, 1024)
    cp = jnp.zeros((rows, d), jnp.float32).at[: c.shape[0]].set(c)
    return pl.pallas_call(
        _ada_kernel,
        grid=(depth, n // tn),
        in_specs=[
            pl.BlockSpec((rows, d), lambda l, j: (0, 0)),
            pl.BlockSpec((None, d, tn), lambda l, j: (l, 0, j)),
            pl.BlockSpec((None, 1, tn), lambda l, j: (l, 0, j)),
        ],
        out_specs=pl.BlockSpec((None, rows, tn), lambda l, j: (l, 0, j)),
        out_shape=jax.ShapeDtypeStruct((depth, rows, n), jnp.float32),
        compiler_params=_cparams(2),
        name="ada_mod",
    )(cp, w_ada, b_ada.reshape(depth, 1, n))


def _rms_mod(x, scale, shift):
    ms = jnp.mean(x * x, axis=-1, keepdims=True)
    return x * lax.rsqrt(ms + NORM_EPS) * (1.0 + scale) + shift


def _cast_rows(src_ref, dst_ref):
    rows_per_step = math.gcd(src_ref.shape[0], 256)
    n = src_ref.shape[0] // rows_per_step

    def body(i, carry):
        r = pl.multiple_of(i * rows_per_step, rows_per_step)
        dst_ref[pl.ds(r, rows_per_step), :] = src_ref[pl.ds(r, rows_per_step), :].astype(dst_ref.dtype)
        return carry

    lax.fori_loop(0, n, body, 0)


def _qkv_kernel(x_ref, shift_ref, scale_ref, w_ref, cos_ref, sin_ref, o_ref, wbf_ref):
    b = pl.program_id(1)
    i = pl.program_id(2)

    @pl.when((b == 0) & (i == 0))
    def _():
        _cast_rows(w_ref, wbf_ref)

    h = _rms_mod(x_ref[...], scale_ref[...], shift_ref[...]).astype(jnp.bfloat16)
    acc = jnp.dot(h, wbf_ref[...], preferred_element_type=jnp.float32)
    cos = cos_ref[...]
    sin = sin_ref[...]
    lane = lax.broadcasted_iota(jnp.int32, cos.shape, 1)
    first_half = (lane % HEAD_DIM) < (HEAD_DIM // 2)
    for c in range(o_ref.shape[0]):
        a = acc[:, c * LANES:(c + 1) * LANES]
        partner = jnp.where(first_half, pltpu.roll(a, LANES - HEAD_DIM // 2, 1), pltpu.roll(a, HEAD_DIM // 2, 1))
        o_ref[c] = (a * cos + partner * sin).astype(o_ref.dtype)


def _qkv_proj(x, mod6, w_in, layer, cos_t, sin_t, tm=512):
    B, S, D = x.shape
    n3 = w_in.shape[2]
    width = n3 // 3
    pairs = width // LANES
    return pl.pallas_call(
        _qkv_kernel,
        grid=(3, B, S // tm),
        in_specs=[
            pl.BlockSpec((None, tm, D), lambda j, b, i: (b, i, 0)),
            pl.BlockSpec((None, None, 1, D), lambda j, b, i: (b, 0, 0, 0)),
            pl.BlockSpec((None, None, 1, D), lambda j, b, i: (b, 1, 0, 0)),
            pl.BlockSpec((None, D, width), lambda j, b, i: (layer, 0, j), pipeline_mode=pl.Buffered(1)),
            pl.BlockSpec((None, None, tm, LANES), lambda j, b, i: (j, b, i, 0)),
            pl.BlockSpec((None, None, tm, LANES), lambda j, b, i: (j, b, i, 0)),
        ],
        out_specs=pl.BlockSpec((None, pairs, tm, LANES), lambda j, b, i: (b, j, i, 0)),
        out_shape=jax.ShapeDtypeStruct((B, 3 * pairs, S, LANES), jnp.bfloat16),
        scratch_shapes=[pltpu.VMEM((D, width), jnp.bfloat16)],
        compiler_params=_cparams(3),
        name="qkv_proj",
    )(x, mod6, mod6, w_in, cos_t, sin_t)


def _dilated_kernel(q_ref, k_ref, v_ref, b1_ref, b4_ref, b16_ref, o_ref, m_ref, l_ref, acc_ref):
    nb = q_ref.shape[0]

    def rows(x):
        return x.reshape(-1, LANES)

    def update(items):
        old = [(acc_ref[idx], m_ref[(0,) + idx], m_ref[(1,) + idx], l_ref[(0,) + idx], l_ref[(1,) + idx])
               for (_, _, _, _, idx) in items]
        new = []
        for (q, kk, vv, bias, idx), (a_raw, m0, m1, l0, l1) in zip(items, old):
            s = _scores(_stack_heads(q), kk) + bias
            m_old = jnp.concatenate([rows(m0), rows(m1)], axis=0)
            l_old = jnp.concatenate([rows(l0), rows(l1)], axis=0)
            a_old = jnp.concatenate([rows(a_raw), rows(a_raw)], axis=0)
            new.append(_softmax_step(s, vv, m_old, l_old, a_old) + (a_raw.shape, q.shape[0]))
        for (_, _, _, _, idx), (m_new, l_new, a_new, shp, n) in zip(items, new):
            m_ref[(0,) + idx] = m_new[:n].reshape(shp)
            m_ref[(1,) + idx] = m_new[n:].reshape(shp)
            l_ref[(0,) + idx] = l_new[:n].reshape(shp)
            l_ref[(1,) + idx] = l_new[n:].reshape(shp)
            acc_ref[idx] = _merge_heads(a_new).reshape(shp)

    def init(n, carry):
        m_ref[0, n] = jnp.full(m_ref.shape[2:], NEG, jnp.float32)
        m_ref[1, n] = jnp.full(m_ref.shape[2:], NEG, jnp.float32)
        l_ref[0, n] = jnp.zeros(l_ref.shape[2:], jnp.float32)
        l_ref[1, n] = jnp.zeros(l_ref.shape[2:], jnp.float32)
        acc_ref[n] = jnp.zeros(acc_ref.shape[1:], jnp.float32)
        return carry

    lax.fori_loop(0, nb, init, 0)

    def run(n_blocks, group, make_item):
        def step(t, carry):
            update([make_item(group * t + g) for g in range(group)])
            return carry

        lax.fori_loop(0, n_blocks // group, step, 0)

    def d1(n):
        prev = jnp.maximum(n - 1, 0)
        kk = jnp.concatenate([rows(k_ref[prev]), rows(k_ref[n])], axis=0)
        vv = jnp.concatenate([rows(v_ref[prev]), rows(v_ref[n])], axis=0)
        return rows(q_ref[n]), kk, vv, b1_ref[jnp.minimum(n, 1)], (n,)

    run(nb, 4, d1)

    per4 = nb // 2

    def d4(t):
        r4 = t // per4
        a = t % per4
        cur = pl.ds(2 * a, 2)
        prev = pl.ds(jnp.maximum(2 * a - 2, 0), 2)
        grp = pl.ds(4 * r4, 4)
        kk = jnp.concatenate([rows(k_ref[prev, grp]), rows(k_ref[cur, grp])], axis=0)
        vv = jnp.concatenate([rows(v_ref[prev, grp]), rows(v_ref[cur, grp])], axis=0)
        return rows(q_ref[cur, grp]), kk, vv, b4_ref[jnp.minimum(a, 1)], (cur, grp)

    run(4 * per4, 16, d4)

    per16 = nb // 8

    def d16(t):
        rr = t // per16
        b = t % per16
        cur = pl.ds(8 * b, 8)
        prev = pl.ds(jnp.maximum(8 * b - 8, 0), 8)
        kk = jnp.concatenate([rows(k_ref[prev, rr]), rows(k_ref[cur, rr])], axis=0)
        vv = jnp.concatenate([rows(v_ref[prev, rr]), rows(v_ref[cur, rr])], axis=0)
        return rows(q_ref[cur, rr]), kk, vv, b16_ref[jnp.minimum(b, 1)], (cur, rr)

    run(16 * per16, 16, d16)

    def fin(n, carry):
        lane = lax.broadcasted_iota(jnp.int32, acc_ref.shape[1:], 2)
        den = jnp.where(lane < HEAD_DIM, l_ref[0, n], l_ref[1, n])
        o_ref[n] = (acc_ref[n] / den).astype(o_ref.dtype)
        return carry

    lax.fori_loop(0, nb, fin, 0)


def _seq_view(a):
    B, P, S, L = a.shape
    return a.reshape(B, P, S // BLK, 16, 16, L)


def _dilated_attention(qkv, biases):
    B, P3, S, _ = qkv.shape
    P = P3 // 3
    pa = P // 2
    nb = S // BLK
    assert S % (16 * BAND) == 0, "sequence must be a whole number of dilation-16 band blocks"
    v6 = _seq_view(qkv)
    blk = (None, None, nb, 16, 16, LANES)
    b1, b4, b16 = biases
    out = pl.pallas_call(
        _dilated_kernel,
        grid=(B, pa),
        in_specs=[
            pl.BlockSpec(blk, lambda b, p: (b, p, 0, 0, 0, 0)),
            pl.BlockSpec(blk, lambda b, p: (b, P + p, 0, 0, 0, 0)),
            pl.BlockSpec(blk, lambda b, p: (b, 2 * P + p, 0, 0, 0, 0)),
            pl.BlockSpec(b1.shape, lambda b, p: (0, 0, 0)),
            pl.BlockSpec(b4.shape, lambda b, p: (0, 0, 0)),
            pl.BlockSpec(b16.shape, lambda b, p: (0, 0, 0)),
        ],
        out_specs=pl.BlockSpec(blk, lambda b, p: (b, p, 0, 0, 0, 0)),
        out_shape=jax.ShapeDtypeStruct((B, pa, nb, 16, 16, LANES), jnp.bfloat16),
        scratch_shapes=[
            pltpu.VMEM((2, nb, 16, 16, LANES), jnp.float32),
            pltpu.VMEM((2, nb, 16, 16, LANES), jnp.float32),
            pltpu.VMEM((nb, 16, 16, LANES), jnp.float32),
        ],
        compiler_params=_cparams(2),
        name="dilated_attn",
    )(v6, v6, v6, b1, b4, b16)
    return out.reshape(B, pa, S, LANES)


def _moba_kernel(q_ref, k_ref, v_ref, own_ref, hot_ref, o_ref, kmean_ref, qa_ref, m_ref, l_ref, acc_ref):
    m = pl.program_id(2)
    nb = k_ref.shape[0]
    nbp = -(-nb // 8) * 8

    def rows(x):
        return x.reshape(-1, LANES)

    @pl.when(m == 0)
    def _():
        kmean_ref[...] = jnp.zeros(kmean_ref.shape, jnp.float32)
        for j in range(nb):
            kmean_ref[pl.ds(j, 1), :] = jnp.mean(rows(k_ref[j]).astype(jnp.float32), axis=0, keepdims=True)

    qs = jnp.concatenate([_stack_heads(rows(q_ref[0])), _stack_heads(rows(q_ref[1]))], axis=0)
    nq = qs.shape[0]
    half = nq // 2

    gate = _scores(kmean_ref[...].astype(jnp.bfloat16), qs)[:nbp]
    jidx = lax.broadcasted_iota(jnp.int32, gate.shape, 0)
    own = 2 * m + lax.broadcasted_iota(jnp.int32, gate.shape, 1) // half
    g = jnp.where(jidx < own, gate, -jnp.inf)
    rank = jnp.zeros(gate.shape, jnp.int32)
    for jp in range(nb):
        other = g[jp:jp + 1, :]
        beats = (other > g) | ((other == g) & (jp < jidx))
        rank = rank + beats.astype(jnp.int32)
    keep = (rank < MOBA_TOPK) & (jidx < own)
    pen_t = jnp.where(keep, 0.0, NEG).astype(jnp.float32)
    pen_t = jnp.concatenate([pen_t, jnp.full((LANES - nbp, nq), NEG, jnp.float32)], axis=0)
    qa_ref[...] = jnp.concatenate([qs, pen_t.T.astype(qs.dtype)], axis=1)

    ka, kb = rows(k_ref[2 * m]), rows(k_ref[2 * m + 1])
    s = jnp.concatenate([_scores(qs[:half], ka), _scores(qs[half:], kb)], axis=0) + own_ref[...]
    m0 = jnp.max(s, axis=1, keepdims=True)
    p = jnp.exp(s - m0)
    m_ref[...] = jnp.broadcast_to(m0, m_ref.shape)
    l_ref[...] = jnp.broadcast_to(jnp.sum(p, axis=1, keepdims=True), l_ref.shape)
    pb = p.astype(jnp.bfloat16)
    acc_ref[...] = jnp.concatenate(
        [jnp.dot(pb[:half], rows(v_ref[2 * m]), preferred_element_type=jnp.float32),
         jnp.dot(pb[half:], rows(v_ref[2 * m + 1]), preferred_element_type=jnp.float32)], axis=0)

    hi = pl.ds(half, half)
    sj = _scores(qa_ref[hi, :], jnp.concatenate([ka, hot_ref[2 * m]], axis=1))
    m_new, l_new, a_new = _softmax_step(sj, rows(v_ref[2 * m]), m_ref[hi, :], l_ref[hi, :], acc_ref[hi, :])
    m_ref[hi, :] = m_new
    l_ref[hi, :] = l_new
    acc_ref[hi, :] = a_new

    def past(t, carry):
        kk = jnp.concatenate([jnp.concatenate([rows(k_ref[2 * t]), hot_ref[2 * t]], axis=1),
                              jnp.concatenate([rows(k_ref[2 * t + 1]), hot_ref[2 * t + 1]], axis=1)], axis=0)
        vv = jnp.concatenate([rows(v_ref[2 * t]), rows(v_ref[2 * t + 1])], axis=0)
        m_new, l_new, a_new = _softmax_step(_scores(qa_ref[...], kk), vv, m_ref[...], l_ref[...], acc_ref[...])
        m_ref[...] = m_new
        l_ref[...] = l_new
        acc_ref[...] = a_new
        return carry

    lax.fori_loop(0, m, past, 0)

    out = acc_ref[...] / l_ref[...]
    o_ref[0] = _merge_heads(out[:half]).reshape(o_ref.shape[1:]).astype(o_ref.dtype)
    o_ref[1] = _merge_heads(out[half:]).reshape(o_ref.shape[1:]).astype(o_ref.dtype)


def _block_one_hot(nb):
    hot = np.zeros((nb, BLK, LANES), np.float32)
    hot[np.arange(nb), :, np.arange(nb)] = 1.0
    return jnp.asarray(hot, jnp.bfloat16)


def _moba_attention(qkv, own_bias):
    B, P3, S, _ = qkv.shape
    P = P3 // 3
    pa = P // 2
    nb = S // BLK
    assert nb <= LANES
    v6 = _seq_view(qkv)
    hot = _block_one_hot(nb)
    full = (None, None, nb, 16, 16, LANES)
    tile = (None, None, 2, 16, 16, LANES)
    nq = 4 * BLK
    out = pl.pallas_call(
        _moba_kernel,
        grid=(B, pa, nb // 2),
        in_specs=[
            pl.BlockSpec(tile, lambda b, p, n: (b, pa + p, n, 0, 0, 0)),
            pl.BlockSpec(full, lambda b, p, n: (b, P + pa + p, 0, 0, 0, 0)),
            pl.BlockSpec(full, lambda b, p, n: (b, 2 * P + pa + p, 0, 0, 0, 0)),
            pl.BlockSpec(own_bias.shape, lambda b, p, n: (0, 0)),
            pl.BlockSpec(hot.shape, lambda b, p, n: (0, 0, 0)),
        ],
        out_specs=pl.BlockSpec(tile, lambda b, p, n: (b, p, n, 0, 0, 0)),
        out_shape=jax.ShapeDtypeStruct((B, pa, nb, 16, 16, LANES), jnp.bfloat16),
        scratch_shapes=[
            pltpu.VMEM((LANES, LANES), jnp.float32),
            pltpu.VMEM((nq, 2 * LANES), jnp.bfloat16),
            pltpu.VMEM((nq, LANES), jnp.float32),
            pltpu.VMEM((nq, LANES), jnp.float32),
            pltpu.VMEM((nq, LANES), jnp.float32),
        ],
        compiler_params=_cparams(3),
        name="moba_attn",
    )(v6, v6, v6, own_bias, hot)
    return out.reshape(B, pa, S, LANES)


def _post_kernel(oa_ref, ob_ref, x_ref, gate_a_ref, shift_f_ref, scale_f_ref, beta_ref, w_ref, wr_ref, br_ref,
                 x1_ref, h2_ref, te_ref, tg_ref, wbf_ref, y_ref, whi_ref, wlo_ref):
    b = pl.program_id(0)
    i = pl.program_id(1)

    @pl.when((b == 0) & (i == 0))
    def _():
        _cast_rows(w_ref, wbf_ref)
        wr = wr_ref[...]
        hi = wr.astype(jnp.bfloat16)
        whi_ref[...] = hi
        wlo_ref[...] = (wr - hi.astype(jnp.float32)).astype(jnp.bfloat16)

    half = oa_ref.shape[0] * LANES
    for off, o_ref in ((0, oa_ref), (half, ob_ref)):
        ss = None
        for c in range(o_ref.shape[0]):
            o = o_ref[c].astype(jnp.float32)
            part = jnp.sum(o * o, axis=-1, keepdims=True)
            ss = part if ss is None else ss + part
        inv = lax.rsqrt(ss / half + NORM_EPS)
        for c in range(o_ref.shape[0]):
            lo = off + c * LANES
            y_ref[:, lo:lo + LANES] = (o_ref[c].astype(jnp.float32) * inv * beta_ref[:, lo:lo + LANES]).astype(y_ref.dtype)

    mixed = jnp.dot(y_ref[...], wbf_ref[...], preferred_element_type=jnp.float32)
    x1 = x_ref[...] + gate_a_ref[...] * mixed
    x1_ref[...] = x1
    h2 = _rms_mod(x1, scale_f_ref[...], shift_f_ref[...])
    h2_ref[...] = h2

    h_hi = h2.astype(jnp.bfloat16)
    h_lo = (h2 - h_hi.astype(jnp.float32)).astype(jnp.bfloat16)
    logits = (jnp.dot(h_hi, whi_ref[...], preferred_element_type=jnp.float32)
              + jnp.dot(h_hi, wlo_ref[...], preferred_element_type=jnp.float32)
              + jnp.dot(h_lo, whi_ref[...], preferred_element_type=jnp.float32)) + br_ref[...]
    lane = lax.broadcasted_iota(jnp.int32, logits.shape, 1).astype(jnp.float32)
    te = jnp.zeros(logits.shape, jnp.float32)
    tg = jnp.zeros(logits.shape, jnp.float32)
    work = logits
    top0 = None
    den = None
    for kk in range(TOP_K):
        mv = jnp.max(work, axis=1, keepdims=True)
        idx = jnp.min(jnp.where(work == mv, lane, float(LANES)), axis=1, keepdims=True)
        work = jnp.where(lane == idx, -jnp.inf, work)
        if kk == 0:
            top0 = mv
        e = jnp.exp(mv - top0)
        den = e if den is None else den + e
        te = jnp.where(lane == kk, idx, te)
        tg = jnp.where(lane == kk, e, tg)
    te_ref[...] = te.astype(jnp.int32)
    tg_ref[...] = tg / den


def _post_attention(oa, ob, x, mod6, beta, w_out, layer, w_router, b_router, tm=256):
    B, S, D = x.shape
    pa = oa.shape[1]
    ne = w_router.shape[1]
    assert TOP_K <= ne <= LANES
    w_router = jnp.pad(w_router, ((0, 0), (0, LANES - ne)))
    b_router = jnp.pad(b_router, (0, LANES - ne), constant_values=-jnp.inf)
    ne = LANES
    row = lambda k: pl.BlockSpec((None, None, 1, D), lambda b, i, k=k: (b, k, 0, 0))
    tok = pl.BlockSpec((None, tm, D), lambda b, i: (b, i, 0))
    small = pl.BlockSpec((None, tm, LANES), lambda b, i: (b, i, 0))
    return pl.pallas_call(
        _post_kernel,
        grid=(B, S // tm),
        in_specs=[
            pl.BlockSpec((None, pa, tm, LANES), lambda b, i: (b, 0, i, 0)),
            pl.BlockSpec((None, pa, tm, LANES), lambda b, i: (b, 0, i, 0)),
            tok, row(2), row(3), row(4),
            pl.BlockSpec((1, D), lambda b, i: (0, 0)),
            pl.BlockSpec((None,) + w_out.shape[1:], lambda b, i: (layer, 0, 0), pipeline_mode=pl.Buffered(1)),
            pl.BlockSpec(w_router.shape, lambda b, i: (0, 0)),
            pl.BlockSpec((1, ne), lambda b, i: (0, 0)),
        ],
        out_specs=[tok, tok, small, small],
        out_shape=[
            jax.ShapeDtypeStruct((B, S, D), jnp.float32),
            jax.ShapeDtypeStruct((B, S, D), jnp.float32),
            jax.ShapeDtypeStruct((B, S, LANES), jnp.int32),
            jax.ShapeDtypeStruct((B, S, LANES), jnp.float32),
        ],
        scratch_shapes=[pltpu.VMEM(w_out.shape[1:], jnp.bfloat16), pltpu.VMEM((tm, D), jnp.bfloat16),
                        pltpu.VMEM(w_router.shape, jnp.bfloat16), pltpu.VMEM(w_router.shape, jnp.bfloat16)],
        compiler_params=_cparams(2),
        name="post_attn_router",
    )(oa, ob, x, mod6, mod6, mod6, beta, w_out, w_router, b_router.reshape(1, ne))


def _route(top_e, n_experts):
    T = top_e.shape[0]
    A = T * TOP_K
    onehot = (top_e[:, :, None] == jnp.arange(n_experts, dtype=jnp.int32)[None, None, :]).astype(jnp.int32)
    tot = jnp.sum(onehot, axis=1)
    incl = jnp.cumsum(tot, axis=0)
    counts = incl[-1]
    padded = ((counts + EXPERT_TILE - 1) // EXPERT_TILE) * EXPERT_TILE
    pend = jnp.cumsum(padded)
    pstart = pend - padded
    dest = jnp.sum(onehot * (incl - tot + pstart[None, :])[:, None, :], axis=2).astype(jnp.int32)
    n_slots = A + n_experts * EXPERT_TILE
    n_blocks = n_slots // EXPERT_TILE
    n_used = (pend[-1] // EXPERT_TILE).astype(jnp.int32)
    blk = jnp.arange(n_blocks, dtype=jnp.int32)
    be = jnp.sum((pend[None, :] <= (blk * EXPERT_TILE)[:, None]).astype(jnp.int32), axis=1)
    be = jnp.minimum(be, n_experts - 1)
    last = jnp.sum(jnp.where(blk == n_used - 1, be, 0))
    be = jnp.where(blk < n_used, be, last)
    group_end = jnp.sum(jnp.where(be[:, None] == jnp.arange(n_experts)[None, :], pend[None, :], 0), axis=1)
    after = group_end // EXPERT_TILE
    be_after = jnp.sum(jnp.where(blk[None, :] == after[:, None], be[None, :], 0), axis=1)
    nxt = jnp.where(after < n_used, be_after, -1).astype(jnp.int32)
    return dest, pend.astype(jnp.int32), be, nxt, n_used.reshape(1)


def _tile_major(dest, tm):
    T = dest.shape[0]
    return dest.reshape(T // tm, tm, TOP_K).transpose(0, 2, 1).reshape(T // tm, 1, TOP_K * tm)


def _issue_pairs(n, make_copy, wait=False):
    if not wait:
        for r in range(n):
            make_copy(r).start(priority=r % 2)
        return

    def body(u, c):
        make_copy(2 * u).wait()
        make_copy(2 * u + 1).wait()
        return c

    lax.fori_loop(0, n // 2, body, 0, unroll=4)


def _dispatch_kernel(pend_ref, dest_ref, h_ref, xs_hbm, zbuf, sem):
    i = pl.program_id(0)
    tm = dest_ref.shape[1] // TOP_K
    tile = zbuf.shape[0]
    ne = pend_ref.shape[0]

    @pl.when(i == 0)
    def _():
        zbuf[...] = jnp.zeros(zbuf.shape, zbuf.dtype)

        def fill(e):
            end = pend_ref[e]
            start = jnp.where(e == 0, 0, pend_ref[jnp.maximum(e - 1, 0)])
            last = pl.multiple_of(jnp.maximum(end - tile, 0), tile)
            return end > start, pltpu.make_async_copy(zbuf, xs_hbm.at[pl.ds(last, tile)], sem)

        def start(e, c):
            ok, cp = fill(e)

            @pl.when(ok)
            def _():
                cp.start()
            return c

        def wait(e, c):
            ok, cp = fill(e)

            @pl.when(ok)
            def _():
                cp.wait()
            return c

        lax.fori_loop(0, ne, start, 0)
        lax.fori_loop(0, ne, wait, 0)

        def tail(b):
            off = pl.multiple_of(b * tile, tile)
            return off >= pend_ref[ne - 1], pltpu.make_async_copy(zbuf, xs_hbm.at[pl.ds(off, tile)], sem)

        def tail_start(b, c):
            ok, cp = tail(b)

            @pl.when(ok)
            def _():
                cp.start()
            return c

        def tail_wait(b, c):
            ok, cp = tail(b)

            @pl.when(ok)
            def _():
                cp.wait()
            return c

        n_tiles = xs_hbm.shape[0] // tile
        lax.fori_loop(0, n_tiles, tail_start, 0)
        lax.fori_loop(0, n_tiles, tail_wait, 0)

    def row_copy(r):
        return pltpu.make_async_copy(h_ref.at[pl.ds(r % tm, 1)], xs_hbm.at[pl.ds(dest_ref[0, r], 1)], sem)

    _issue_pairs(TOP_K * tm, row_copy)
    _issue_pairs(TOP_K * tm, row_copy, wait=True)


def _dispatch(h2, dest, pend, n_slots, tm=256):
    T, D = h2.shape
    grid_spec = pltpu.PrefetchScalarGridSpec(
        num_scalar_prefetch=1,
        grid=(T // tm,),
        in_specs=[
            pl.BlockSpec((None, 1, TOP_K * tm), lambda i, pe: (i, 0, 0), memory_space=pltpu.SMEM),
            pl.BlockSpec((tm, D), lambda i, pe: (i, 0)),
        ],
        out_specs=pl.BlockSpec(memory_space=pl.ANY),
        scratch_shapes=[pltpu.VMEM((EXPERT_TILE, D), h2.dtype), pltpu.SemaphoreType.DMA(())],
    )
    return pl.pallas_call(
        _dispatch_kernel,
        grid_spec=grid_spec,
        out_shape=jax.ShapeDtypeStruct((n_slots, D), h2.dtype),
        compiler_params=_cparams(1),
        name="moe_dispatch",
    )(pend, _tile_major(dest, tm), h2)


def _expert_kernel(be_ref, nxt_ref, nused_ref, x_ref, wu_hbm, bu_ref, wd_hbm, bd_ref, o_ref,
                   wu32, wd32, wub, wdb, sems, *, layer):
    i = pl.program_id(0)
    d_ff = wd32.shape[0]

    def fetch(e):
        return (pltpu.make_async_copy(wu_hbm.at[layer, e], wu32, sems.at[0]),
                pltpu.make_async_copy(wd_hbm.at[layer, e], wd32, sems.at[1]))

    @pl.when(i == 0)
    def _():
        for cp in fetch(be_ref[0]):
            cp.start()

    @pl.when(i < nused_ref[0])
    def _():
        @pl.when((i == 0) | (be_ref[i] != be_ref[jnp.maximum(i - 1, 0)]))
        def _():
            for cp in fetch(be_ref[i]):
                cp.wait()
            _cast_rows(wu32, wub)
            _cast_rows(wd32, wdb)

            @pl.when(nxt_ref[i] >= 0)
            def _():
                for cp in fetch(nxt_ref[i]):
                    cp.start()

        x = x_ref[...].astype(jnp.bfloat16)
        u = jnp.dot(x, wub[...], preferred_element_type=jnp.float32) + bu_ref[...]
        glu = jnp.minimum(u[:, :d_ff], SWIGLU_LIMIT)
        lin = jnp.clip(u[:, d_ff:], -SWIGLU_LIMIT, SWIGLU_LIMIT)
        act = glu * (1.0 / (1.0 + jnp.exp(-SWIGLU_ALPHA * glu))) * (lin + 1.0)
        o_ref[...] = jnp.dot(act.astype(jnp.bfloat16), wdb[...], preferred_element_type=jnp.float32) + bd_ref[...]

    @pl.when(i >= nused_ref[0])
    def _():
        o_ref[...] = jnp.zeros(o_ref.shape, o_ref.dtype)


def _experts(xs, be, nxt, n_used, layer, w_up, b_up, w_down, b_down):
    n_slots, D = xs.shape
    depth, ne, _, f2 = w_up.shape
    d_ff = w_down.shape[2]
    tm = EXPERT_TILE
    n_blocks = n_slots // tm
    grid_spec = pltpu.PrefetchScalarGridSpec(
        num_scalar_prefetch=3,
        grid=(n_blocks,),
        in_specs=[
            pl.BlockSpec((tm, D), lambda i, be, nx, nu: (jnp.minimum(i, nu[0] - 1), 0)),
            pl.BlockSpec(memory_space=pl.ANY),
            pl.BlockSpec((None, None, 1, f2), lambda i, be, nx, nu: (layer, be[i], 0, 0)),
            pl.BlockSpec(memory_space=pl.ANY),
            pl.BlockSpec((None, None, 1, D), lambda i, be, nx, nu: (layer, be[i], 0, 0)),
        ],
        out_specs=pl.BlockSpec((tm, D), lambda i, be, nx, nu: (i, 0)),
        scratch_shapes=[
            pltpu.VMEM((D, f2), jnp.float32),
            pltpu.VMEM((d_ff, D), jnp.float32),
            pltpu.VMEM((D, f2), jnp.bfloat16),
            pltpu.VMEM((d_ff, D), jnp.bfloat16),
            pltpu.SemaphoreType.DMA((2,)),
        ],
    )
    return pl.pallas_call(
        functools.partial(_expert_kernel, layer=layer),
        grid_spec=grid_spec,
        out_shape=jax.ShapeDtypeStruct((n_slots, D), jnp.float32),
        compiler_params=_cparams(1),
        name="experts",
    )(be, nxt, n_used, xs, w_up, b_up.reshape(depth, ne, 1, f2), w_down, b_down.reshape(depth, ne, 1, D))


def _combine_kernel(dest_ref, y_hbm, x_ref, gate_ref, tg_ref, gfin_ref, o_ref, ybuf, sem, *, final):
    tm = x_ref.shape[0]

    def row_copy(r):
        return pltpu.make_async_copy(y_hbm.at[pl.ds(dest_ref[0, r], 1)], ybuf.at[pl.ds(r, 1)], sem)

    _issue_pairs(TOP_K * tm, row_copy)
    _issue_pairs(TOP_K * tm, row_copy, wait=True)
    tg = tg_ref[...]
    y = None
    for k in range(TOP_K):
        part = tg[:, k:k + 1] * ybuf[k * tm:(k + 1) * tm, :]
        y = part if y is None else y + part
    x2 = x_ref[...] + gate_ref[...] * y
    if final:
        ms = jnp.mean(x2 * x2, axis=-1, keepdims=True)
        x2 = x2 * lax.rsqrt(ms + NORM_EPS) * gfin_ref[...]
    o_ref[...] = x2


def _combine(outs, dest, tg, x1, mod6, g_final, final, tm=256):
    B, S, D = x1.shape
    nt = S // tm
    tok = pl.BlockSpec((None, tm, D), lambda b, i: (b, i, 0))
    return pl.pallas_call(
        functools.partial(_combine_kernel, final=final),
        grid=(B, nt),
        in_specs=[
            pl.BlockSpec((None, 1, TOP_K * tm), lambda b, i: (b * nt + i, 0, 0), memory_space=pltpu.SMEM),
            pl.BlockSpec(memory_space=pl.ANY),
            tok,
            pl.BlockSpec((None, None, 1, D), lambda b, i: (b, 5, 0, 0)),
            pl.BlockSpec((None, tm, LANES), lambda b, i: (b, i, 0)),
            pl.BlockSpec((1, D), lambda b, i: (0, 0)),
        ],
        out_specs=tok,
        out_shape=jax.ShapeDtypeStruct((B, S, D), jnp.float32),
        scratch_shapes=[pltpu.VMEM((TOP_K * tm, D), jnp.float32), pltpu.SemaphoreType.DMA(())],
        compiler_params=_cparams(2),
        name="moe_combine",
    )(_tile_major(dest, tm), outs, x1, mod6, tg, g_final.reshape(1, D))


def _rotary_tables(positions, scale_q):
    inv_freq = 1.0 / (ROPE_THETA ** (jnp.arange(0, HEAD_DIM, 2, dtype=jnp.float32) / HEAD_DIM))
    ang = positions[..., None].astype(jnp.float32) * inv_freq
    cos, sin = jnp.cos(ang), jnp.sin(ang)
    reps = LANES // HEAD_DIM
    cos_l = jnp.tile(jnp.concatenate([cos, cos], axis=-1), (1, 1, reps))
    sin_l = jnp.tile(jnp.concatenate([-sin, sin], axis=-1), (1, 1, reps))
    cos_t = jnp.stack([cos_l * scale_q, cos_l, jnp.ones_like(cos_l)])
    sin_t = jnp.stack([sin_l * scale_q, sin_l, jnp.zeros_like(sin_l)])
    return cos_t, sin_t


def kernel(x, c, positions, w_ada, b_ada, w_in, beta_a, beta_b, w_out, w_router, b_router, w_up, b_up, w_down, b_down, g_final):
    B, S, D = x.shape
    depth = w_ada.shape[0]
    ne = w_router.shape[-1]
    biases = tuple(jnp.asarray(b) for b in _dilated_biases())
    own_bias = jnp.asarray(_own_bias())

    xp = _to_perm(x)
    pos_p = _to_perm(positions)
    cos_t, sin_t = _rotary_tables(pos_p, HEAD_DIM ** -0.5)
    mod = _ada_mod(c, w_ada, b_ada)

    for l in range(depth):
        mod6 = mod[l, :B].reshape(B, 6, 1, D)
        qkv = _qkv_proj(xp, mod6, w_in, l, cos_t, sin_t)
        oa = _dilated_attention(qkv, biases)
        ob = _moba_attention(qkv, own_bias)
        beta = jnp.concatenate([beta_a[l], beta_b[l]]).reshape(1, D)
        x1, h2, te, tg = _post_attention(oa, ob, xp, mod6, beta, w_out, l, w_router[l], b_router[l])
        dest, pend, be, nxt, n_used = _route(te.reshape(B * S, LANES)[:, :TOP_K], ne)
        xs = _dispatch(h2.reshape(B * S, D), dest, pend, be.shape[0] * EXPERT_TILE)
        outs = _experts(xs, be, nxt, n_used, l, w_up, b_up, w_down, b_down)
        xp = _combine(outs, dest, tg, x1, mod6, g_final, final=(l == depth - 1))

    return _from_perm(xp)
```

```python
import functools
import math

import numpy as np
import jax
import jax.numpy as jnp
from jax import lax
from jax.experimental import pallas as pl
from jax.experimental.pallas import tpu as pltpu

HEAD_DIM = 64
LANES = 128
BLK = 256
BAND = 128
MOBA_TOPK = 3
TOP_K = 4
SWIGLU_LIMIT = 7.0
SWIGLU_ALPHA = 1.702
ROPE_THETA = 10000.0
NORM_EPS = 1e-6
NEG = -1e30
EXPERT_TILE = 256
QKV_ROWS = 512
POST_ROWS = 256
MOE_ROWS = 256
VMEM_LIMIT = 60000 * 1024

_ARB = pltpu.ARBITRARY


def _cparams(n_axes):
    return pltpu.CompilerParams(dimension_semantics=(_ARB,) * n_axes, vmem_limit_bytes=VMEM_LIMIT)


def _to_perm(x):
    B, S = x.shape[:2]
    rest = x.shape[2:]
    x = x.reshape(B, S // BLK, 16, 4, 4, *rest)
    x = jnp.swapaxes(x, 2, 4)
    return x.reshape(B, S, *rest)


def _permute_kernel(x_ref, o_ref, *, inverse):
    for r4 in range(4):
        for c in range(4):
            natural = pl.ds(4 * c + r4, 16, stride=16)
            permuted = pl.ds(64 * r4 + 16 * c, 16)
            if inverse:
                o_ref[natural] = x_ref[permuted]
            else:
                o_ref[permuted] = x_ref[natural]


def _permute_rows(x, inverse):
    B, S, D = x.shape
    v = x.reshape(B, S, D // LANES, LANES)
    blk = pl.BlockSpec((None, BLK, D // LANES, LANES), lambda b, n: (b, n, 0, 0))
    out = pl.pallas_call(
        functools.partial(_permute_kernel, inverse=inverse),
        grid=(B, S // BLK),
        in_specs=[blk],
        out_specs=blk,
        out_shape=jax.ShapeDtypeStruct(v.shape, x.dtype),
        compiler_params=_cparams(2),
        name="permute_rows",
    )(v)
    return out.reshape(B, S, D)


def _local_pos():
    p = np.arange(BLK)
    return 16 * (p % 16) + 4 * ((p // 16) % 4) + p // 64


def _band_bias(seq_q, seq_k, n_prev=None):
    dist = seq_q[:, None] - seq_k[None, :]
    ok = (dist >= 0) & (dist <= BAND)
    n_prev = seq_k.shape[0] // 2 if n_prev is None else n_prev
    exists = np.arange(seq_k.shape[0])[None, :] >= n_prev
    normal = np.where(ok, 0.0, NEG).astype(np.float32)
    first = np.where(ok & exists, 0.0, NEG).astype(np.float32)
    two = np.stack([first, normal])
    return np.concatenate([two, two], axis=1)


def _dilated_biases():
    lp = _local_pos()
    late = lp.reshape(16, 16)[:, 8:].reshape(-1)
    b1 = _band_bias(lp + BLK, np.concatenate([late, lp + BLK]), n_prev=late.shape[0])
    hi, c, lo = np.meshgrid(np.arange(4), np.arange(4), np.arange(16), indexing="ij")
    seq_k4 = (64 * hi + 4 * lo + c).reshape(-1)
    seq_q4 = seq_k4[:128] + 128
    b4 = _band_bias(seq_q4, seq_k4)
    b16 = _band_bias(np.arange(128) + 128, np.arange(256))
    return b1, b4, b16


def _own_bias():
    lp = _local_pos()
    c = np.where(lp[None, :] <= lp[:, None], 0.0, NEG).astype(np.float32)
    return np.concatenate([c, c, c, c], axis=0)


def _stack_heads(q):
    lane = lax.broadcasted_iota(jnp.int32, q.shape, 1)
    zero = jnp.zeros_like(q)
    return jnp.concatenate([jnp.where(lane < HEAD_DIM, q, zero), jnp.where(lane >= HEAD_DIM, q, zero)], axis=0)


def _scores(qs, kk):
    return lax.dot_general(qs, kk, (((1,), (1,)), ((), ())), preferred_element_type=jnp.float32)


def _tile_lanes(x, k):
    return x if k == 1 else jnp.concatenate([x] * k, axis=1)


def _softmax_step(s, vv, m_old, l_old, acc_old):
    m_cur = jnp.max(s, axis=1, keepdims=True)
    m_new = jnp.maximum(m_old, m_cur)
    alpha = jnp.exp(m_old - m_new)
    p = jnp.exp(s - _tile_lanes(m_new, s.shape[1] // LANES))
    l_new = alpha * l_old + jnp.sum(p, axis=1, keepdims=True)
    pv = jnp.dot(p.astype(vv.dtype), vv, preferred_element_type=jnp.float32)
    return m_new, l_new, alpha * acc_old + pv


def _merge_heads(x):
    n = x.shape[0] // 2
    lane = lax.broadcasted_iota(jnp.int32, (n, LANES), 1)
    return jnp.where(lane < HEAD_DIM, x[:n], x[n:])


def _ada_kernel(c_ref, w_ref, b_ref, o_ref):
    c = c_ref[...]
    sc = (c * (1.0 / (1.0 + jnp.exp(-c)))).astype(jnp.bfloat16)
    o_ref[...] = jnp.dot(sc, w_ref[...].astype(jnp.bfloat16), preferred_element_type=jnp.float32) + b_ref[...]


def _ada_mod(c, w_ada, b_ada):
    depth, d, n = w_ada.shape
    rows = 8
    tn = math.gcd(n, 1024)
    cp = jnp.zeros((rows, d), jnp.float32).at[: c.shape[0]].set(c)
    return pl.pallas_call(
        _ada_kernel,
        grid=(depth, n // tn),
        in_specs=[
            pl.BlockSpec((rows, d), lambda l, j: (0, 0)),
            pl.BlockSpec((None, d, tn), lambda l, j: (l, 0, j)),
            pl.BlockSpec((None, 1, tn), lambda l, j: (l, 0, j)),
        ],
        out_specs=pl.BlockSpec((None, rows, tn), lambda l, j: (l, 0, j)),
        out_shape=jax.ShapeDtypeStruct((depth, rows, n), jnp.float32),
        compiler_params=_cparams(2),
        name="ada_mod",
    )(cp, w_ada, b_ada.reshape(depth, 1, n))


def _rms_mod(x, scale, shift):
    ms = jnp.mean(x * x, axis=-1, keepdims=True)
    return x * lax.rsqrt(ms + NORM_EPS) * (1.0 + scale) + shift


def _cast_rows(src_ref, dst_ref):
    rows_per_step = math.gcd(src_ref.shape[0], 256)
    n = src_ref.shape[0] // rows_per_step

    def body(i, carry):
        r = pl.multiple_of(i * rows_per_step, rows_per_step)
        dst_ref[pl.ds(r, rows_per_step), :] = src_ref[pl.ds(r, rows_per_step), :].astype(dst_ref.dtype)
        return carry

    lax.fori_loop(0, n, body, 0)


def _qkv_kernel(x_ref, shift_ref, scale_ref, w_ref, cos_ref, sin_ref, o_ref, wbf_ref):
    b = pl.program_id(1)
    i = pl.program_id(2)

    @pl.when((b == 0) & (i == 0))
    def _():
        _cast_rows(w_ref, wbf_ref)

    h = _rms_mod(x_ref[...], scale_ref[...], shift_ref[...]).astype(jnp.bfloat16)
    acc = jnp.dot(h, wbf_ref[...], preferred_element_type=jnp.float32)
    cos = cos_ref[...]
    sin = sin_ref[...]
    lane = lax.broadcasted_iota(jnp.int32, cos.shape, 1)
    first_half = (lane % HEAD_DIM) < (HEAD_DIM // 2)
    for c in range(o_ref.shape[0]):
        a = acc[:, c * LANES:(c + 1) * LANES]
        partner = jnp.where(first_half, pltpu.roll(a, LANES - HEAD_DIM // 2, 1), pltpu.roll(a, HEAD_DIM // 2, 1))
        o_ref[c] = (a * cos + partner * sin).astype(o_ref.dtype)


def _qkv_proj(x, mod6, w_in, layer, cos_t, sin_t, tm=QKV_ROWS):
    B, S, D = x.shape
    n3 = w_in.shape[2]
    width = n3 // 3
    pairs = width // LANES
    return pl.pallas_call(
        _qkv_kernel,
        grid=(3, B, S // tm),
        in_specs=[
            pl.BlockSpec((None, tm, D), lambda j, b, i: (b, i, 0)),
            pl.BlockSpec((None, None, 1, D), lambda j, b, i: (b, 0, 0, 0)),
            pl.BlockSpec((None, None, 1, D), lambda j, b, i: (b, 1, 0, 0)),
            pl.BlockSpec((None, D, width), lambda j, b, i: (layer, 0, j), pipeline_mode=pl.Buffered(1)),
            pl.BlockSpec((None, None, tm, LANES), lambda j, b, i: (j, b, i, 0)),
            pl.BlockSpec((None, None, tm, LANES), lambda j, b, i: (j, b, i, 0)),
        ],
        out_specs=pl.BlockSpec((None, pairs, tm, LANES), lambda j, b, i: (b, j, i, 0)),
        out_shape=jax.ShapeDtypeStruct((B, 3 * pairs, S, LANES), jnp.bfloat16),
        scratch_shapes=[pltpu.VMEM((D, width), jnp.bfloat16)],
        compiler_params=_cparams(3),
        name="qkv_proj",
    )(x, mod6, mod6, w_in, cos_t, sin_t)


def _dilated_kernel(q_ref, k_ref, v_ref, b1_ref, b4_ref, b16_ref, o_ref, m_ref, l_ref, acc_ref):
    nb = q_ref.shape[0]

    def rows(x):
        return x.reshape(-1, LANES)

    def update(items):
        old = [(acc_ref[idx], m_ref[(0,) + idx], m_ref[(1,) + idx], l_ref[(0,) + idx], l_ref[(1,) + idx])
               for (_, _, _, _, idx) in items]
        new = []
        for (q, kk, vv, bias, idx), (a_raw, m0, m1, l0, l1) in zip(items, old):
            s = _scores(_stack_heads(q), kk) + bias
            m_old = jnp.concatenate([rows(m0), rows(m1)], axis=0)
            l_old = jnp.concatenate([rows(l0), rows(l1)], axis=0)
            a_old = jnp.concatenate([rows(a_raw), rows(a_raw)], axis=0)
            new.append(_softmax_step(s, vv, m_old, l_old, a_old) + (a_raw.shape, q.shape[0]))
        for (_, _, _, _, idx), (m_new, l_new, a_new, shp, n) in zip(items, new):
            m_ref[(0,) + idx] = m_new[:n].reshape(shp)
            m_ref[(1,) + idx] = m_new[n:].reshape(shp)
            l_ref[(0,) + idx] = l_new[:n].reshape(shp)
            l_ref[(1,) + idx] = l_new[n:].reshape(shp)
            acc_ref[idx] = _merge_heads(a_new).reshape(shp)

    def init(n, carry):
        m_ref[0, n] = jnp.full(m_ref.shape[2:], NEG, jnp.float32)
        m_ref[1, n] = jnp.full(m_ref.shape[2:], NEG, jnp.float32)
        l_ref[0, n] = jnp.zeros(l_ref.shape[2:], jnp.float32)
        l_ref[1, n] = jnp.zeros(l_ref.shape[2:], jnp.float32)
        acc_ref[n] = jnp.zeros(acc_ref.shape[1:], jnp.float32)
        return carry

    lax.fori_loop(0, nb, init, 0)

    def run(n_blocks, group, make_item):
        def step(t, carry):
            update([make_item(group * t + g) for g in range(group)])
            return carry

        lax.fori_loop(0, n_blocks // group, step, 0)

    def d1(n):
        prev = jnp.maximum(n - 1, 0)
        late = pl.ds(8, 8)
        kk = jnp.concatenate([rows(k_ref[prev, :, late, :]), rows(k_ref[n])], axis=0)
        vv = jnp.concatenate([rows(v_ref[prev, :, late, :]), rows(v_ref[n])], axis=0)
        return rows(q_ref[n]), kk, vv, b1_ref[jnp.minimum(n, 1)], (n,)

    run(nb, 4, d1)

    per4 = nb // 2

    def d4(t):
        r4 = t // per4
        a = t % per4
        cur = pl.ds(2 * a, 2)
        prev = pl.ds(jnp.maximum(2 * a - 2, 0), 2)
        grp = pl.ds(4 * r4, 4)
        kk = jnp.concatenate([rows(k_ref[prev, grp]), rows(k_ref[cur, grp])], axis=0)
        vv = jnp.concatenate([rows(v_ref[prev, grp]), rows(v_ref[cur, grp])], axis=0)
        return rows(q_ref[cur, grp]), kk, vv, b4_ref[jnp.minimum(a, 1)], (cur, grp)

    run(4 * per4, 16, d4)

    per16 = nb // 8

    def d16(t):
        rr = t // per16
        b = t % per16
        cur = pl.ds(8 * b, 8)
        prev = pl.ds(jnp.maximum(8 * b - 8, 0), 8)
        kk = jnp.concatenate([rows(k_ref[prev, rr]), rows(k_ref[cur, rr])], axis=0)
        vv = jnp.concatenate([rows(v_ref[prev, rr]), rows(v_ref[cur, rr])], axis=0)
        return rows(q_ref[cur, rr]), kk, vv, b16_ref[jnp.minimum(b, 1)], (cur, rr)

    run(16 * per16, 16, d16)

    def fin(n, carry):
        lane = lax.broadcasted_iota(jnp.int32, acc_ref.shape[1:], 2)
        den = jnp.where(lane < HEAD_DIM, l_ref[0, n], l_ref[1, n])
        o_ref[n] = (acc_ref[n] / den).astype(o_ref.dtype)
        return carry

    lax.fori_loop(0, nb, fin, 0)


def _seq_view(a):
    B, P, S, L = a.shape
    return a.reshape(B, P, S // BLK, 16, 16, L)


def _dilated_attention(qkv, biases):
    B, P3, S, _ = qkv.shape
    P = P3 // 3
    pa = P // 2
    nb = S // BLK
    assert S % (16 * BAND) == 0, "sequence must be a whole number of dilation-16 band blocks"
    v6 = _seq_view(qkv)
    blk = (None, None, nb, 16, 16, LANES)
    b1, b4, b16 = biases
    out = pl.pallas_call(
        _dilated_kernel,
        grid=(B, pa),
        in_specs=[
            pl.BlockSpec(blk, lambda b, p: (b, p, 0, 0, 0, 0)),
            pl.BlockSpec(blk, lambda b, p: (b, P + p, 0, 0, 0, 0)),
            pl.BlockSpec(blk, lambda b, p: (b, 2 * P + p, 0, 0, 0, 0)),
            pl.BlockSpec(b1.shape, lambda b, p: (0, 0, 0)),
            pl.BlockSpec(b4.shape, lambda b, p: (0, 0, 0)),
            pl.BlockSpec(b16.shape, lambda b, p: (0, 0, 0)),
        ],
        out_specs=pl.BlockSpec(blk, lambda b, p: (b, p, 0, 0, 0, 0)),
        out_shape=jax.ShapeDtypeStruct((B, pa, nb, 16, 16, LANES), jnp.bfloat16),
        scratch_shapes=[
            pltpu.VMEM((2, nb, 16, 16, LANES), jnp.float32),
            pltpu.VMEM((2, nb, 16, 16, LANES), jnp.float32),
            pltpu.VMEM((nb, 16, 16, LANES), jnp.float32),
        ],
        compiler_params=_cparams(2),
        name="dilated_attn",
    )(v6, v6, v6, b1, b4, b16)
    return out.reshape(B, pa, S, LANES)


def _moba_kernel(q_ref, k_ref, v_ref, own_ref, hot_ref, o_ref, kmean_ref, qa_ref, m_ref, l_ref, acc_ref):
    m = pl.program_id(2)
    nb = k_ref.shape[0]
    nbp = -(-nb // 8) * 8

    def rows(x):
        return x.reshape(-1, LANES)

    @pl.when(m == 0)
    def _():
        kmean_ref[...] = jnp.zeros(kmean_ref.shape, jnp.float32)
        for j in range(nb):
            kmean_ref[pl.ds(j, 1), :] = jnp.mean(rows(k_ref[j]).astype(jnp.float32), axis=0, keepdims=True)

    qs = jnp.concatenate([_stack_heads(rows(q_ref[0])), _stack_heads(rows(q_ref[1]))], axis=0)
    nq = qs.shape[0]
    half = nq // 2

    gate = _scores(kmean_ref[...].astype(jnp.bfloat16), qs)[:nbp]
    jidx = lax.broadcasted_iota(jnp.int32, gate.shape, 0)
    own = 2 * m + lax.broadcasted_iota(jnp.int32, gate.shape, 1) // half
    g = jnp.where(jidx < own, gate, -jnp.inf)
    rank = jnp.zeros(gate.shape, jnp.int32)
    for jp in range(nb):
        other = g[jp:jp + 1, :]
        beats = (other > g) | ((other == g) & (jp < jidx))
        rank = rank + beats.astype(jnp.int32)
    keep = (rank < MOBA_TOPK) & (jidx < own)
    pen_t = jnp.where(keep, 0.0, NEG).astype(jnp.float32)
    pen_t = jnp.concatenate([pen_t, jnp.full((LANES - nbp, nq), NEG, jnp.float32)], axis=0)
    qa_ref[...] = jnp.concatenate([qs, pen_t.T.astype(qs.dtype)], axis=1)

    ka, kb = rows(k_ref[2 * m]), rows(k_ref[2 * m + 1])
    s = jnp.concatenate([_scores(qs[:half], ka), _scores(qs[half:], kb)], axis=0) + own_ref[...]
    m0 = jnp.max(s, axis=1, keepdims=True)
    p = jnp.exp(s - m0)
    m_ref[...] = jnp.broadcast_to(m0, m_ref.shape)
    l_ref[...] = jnp.broadcast_to(jnp.sum(p, axis=1, keepdims=True), l_ref.shape)
    pb = p.astype(jnp.bfloat16)
    acc_ref[...] = jnp.concatenate(
        [jnp.dot(pb[:half], rows(v_ref[2 * m]), preferred_element_type=jnp.float32),
         jnp.dot(pb[half:], rows(v_ref[2 * m + 1]), preferred_element_type=jnp.float32)], axis=0)

    hi = pl.ds(half, half)
    sj = _scores(qa_ref[hi, :], jnp.concatenate([ka, hot_ref[2 * m]], axis=1))
    m_new, l_new, a_new = _softmax_step(sj, rows(v_ref[2 * m]), m_ref[hi, :], l_ref[hi, :], acc_ref[hi, :])
    m_ref[hi, :] = m_new
    l_ref[hi, :] = l_new
    acc_ref[hi, :] = a_new

    def past(t, carry):
        kk = jnp.concatenate([jnp.concatenate([rows(k_ref[2 * t]), hot_ref[2 * t]], axis=1),
                              jnp.concatenate([rows(k_ref[2 * t + 1]), hot_ref[2 * t + 1]], axis=1)], axis=0)
        vv = jnp.concatenate([rows(v_ref[2 * t]), rows(v_ref[2 * t + 1])], axis=0)
        m_new, l_new, a_new = _softmax_step(_scores(qa_ref[...], kk), vv, m_ref[...], l_ref[...], acc_ref[...])
        m_ref[...] = m_new
        l_ref[...] = l_new
        acc_ref[...] = a_new
        return carry

    lax.fori_loop(0, m, past, 0)

    out = acc_ref[...] / l_ref[...]
    o_ref[0] = _merge_heads(out[:half]).reshape(o_ref.shape[1:]).astype(o_ref.dtype)
    o_ref[1] = _merge_heads(out[half:]).reshape(o_ref.shape[1:]).astype(o_ref.dtype)


def _block_one_hot(nb):
    hot = np.zeros((nb, BLK, LANES), np.float32)
    hot[np.arange(nb), :, np.arange(nb)] = 1.0
    return jnp.asarray(hot, jnp.bfloat16)


def _moba_attention(qkv, own_bias):
    B, P3, S, _ = qkv.shape
    P = P3 // 3
    pa = P // 2
    nb = S // BLK
    assert nb <= LANES
    v6 = _seq_view(qkv)
    hot = _block_one_hot(nb)
    full = (None, None, nb, 16, 16, LANES)
    tile = (None, None, 2, 16, 16, LANES)
    nq = 4 * BLK
    out = pl.pallas_call(
        _moba_kernel,
        grid=(B, pa, nb // 2),
        in_specs=[
            pl.BlockSpec(tile, lambda b, p, n: (b, pa + p, n, 0, 0, 0)),
            pl.BlockSpec(full, lambda b, p, n: (b, P + pa + p, 0, 0, 0, 0)),
            pl.BlockSpec(full, lambda b, p, n: (b, 2 * P + pa + p, 0, 0, 0, 0)),
            pl.BlockSpec(own_bias.shape, lambda b, p, n: (0, 0)),
            pl.BlockSpec(hot.shape, lambda b, p, n: (0, 0, 0)),
        ],
        out_specs=pl.BlockSpec(tile, lambda b, p, n: (b, p, n, 0, 0, 0)),
        out_shape=jax.ShapeDtypeStruct((B, pa, nb, 16, 16, LANES), jnp.bfloat16),
        scratch_shapes=[
            pltpu.VMEM((LANES, LANES), jnp.float32),
            pltpu.VMEM((nq, 2 * LANES), jnp.bfloat16),
            pltpu.VMEM((nq, LANES), jnp.float32),
            pltpu.VMEM((nq, LANES), jnp.float32),
            pltpu.VMEM((nq, LANES), jnp.float32),
        ],
        compiler_params=_cparams(3),
        name="moba_attn",
    )(v6, v6, v6, own_bias, hot)
    return out.reshape(B, pa, S, LANES)


def _post_kernel(oa_ref, ob_ref, x_ref, gate_a_ref, shift_f_ref, scale_f_ref, beta_ref, w_ref, wr_ref, br_ref,
                 x1_ref, h2_ref, te_ref, tg_ref, wbf_ref, y_ref, whi_ref, wlo_ref):
    b = pl.program_id(0)
    i = pl.program_id(1)

    @pl.when((b == 0) & (i == 0))
    def _():
        _cast_rows(w_ref, wbf_ref)
        wr = wr_ref[...]
        hi = wr.astype(jnp.bfloat16)
        whi_ref[...] = hi
        wlo_ref[...] = (wr - hi.astype(jnp.float32)).astype(jnp.bfloat16)

    half = oa_ref.shape[0] * LANES
    for off, o_ref in ((0, oa_ref), (half, ob_ref)):
        ss = None
        for c in range(o_ref.shape[0]):
            o = o_ref[c].astype(jnp.float32)
            part = jnp.sum(o * o, axis=-1, keepdims=True)
            ss = part if ss is None else ss + part
        inv = lax.rsqrt(ss / half + NORM_EPS)
        for c in range(o_ref.shape[0]):
            lo = off + c * LANES
            y_ref[:, lo:lo + LANES] = (o_ref[c].astype(jnp.float32) * inv * beta_ref[:, lo:lo + LANES]).astype(y_ref.dtype)

    mixed = jnp.dot(y_ref[...], wbf_ref[...], preferred_element_type=jnp.float32)
    x1 = x_ref[...] + gate_a_ref[...] * mixed
    x1_ref[...] = x1
    h2 = _rms_mod(x1, scale_f_ref[...], shift_f_ref[...])
    h2_ref[...] = h2

    h_hi = h2.astype(jnp.bfloat16)
    h_lo = (h2 - h_hi.astype(jnp.float32)).astype(jnp.bfloat16)
    logits = (jnp.dot(h_hi, whi_ref[...], preferred_element_type=jnp.float32)
              + jnp.dot(h_hi, wlo_ref[...], preferred_element_type=jnp.float32)
              + jnp.dot(h_lo, whi_ref[...], preferred_element_type=jnp.float32)) + br_ref[...]
    lane = lax.broadcasted_iota(jnp.int32, logits.shape, 1).astype(jnp.float32)
    te = jnp.zeros(logits.shape, jnp.float32)
    tg = jnp.zeros(logits.shape, jnp.float32)
    work = logits
    top0 = None
    den = None
    for kk in range(TOP_K):
        mv = jnp.max(work, axis=1, keepdims=True)
        idx = jnp.min(jnp.where(work == mv, lane, float(LANES)), axis=1, keepdims=True)
        work = jnp.where(lane == idx, -jnp.inf, work)
        if kk == 0:
            top0 = mv
        e = jnp.exp(mv - top0)
        den = e if den is None else den + e
        te = jnp.where(lane == kk, idx, te)
        tg = jnp.where(lane == kk, e, tg)
    te_ref[...] = te.astype(jnp.int32)
    tg_ref[...] = tg / den


def _post_attention(oa, ob, x, mod6, beta, w_out, layer, w_router, b_router, tm=POST_ROWS):
    B, S, D = x.shape
    pa = oa.shape[1]
    ne = w_router.shape[1]
    assert TOP_K <= ne <= LANES
    w_router = jnp.pad(w_router, ((0, 0), (0, LANES - ne)))
    b_router = jnp.pad(b_router, (0, LANES - ne), constant_values=-jnp.inf)
    ne = LANES
    row = lambda k: pl.BlockSpec((None, None, 1, D), lambda b, i, k=k: (b, k, 0, 0))
    tok = pl.BlockSpec((None, tm, D), lambda b, i: (b, i, 0))
    small = pl.BlockSpec((None, tm, LANES), lambda b, i: (b, i, 0))
    return pl.pallas_call(
        _post_kernel,
        grid=(B, S // tm),
        in_specs=[
            pl.BlockSpec((None, pa, tm, LANES), lambda b, i: (b, 0, i, 0)),
            pl.BlockSpec((None, pa, tm, LANES), lambda b, i: (b, 0, i, 0)),
            tok, row(2), row(3), row(4),
            pl.BlockSpec((1, D), lambda b, i: (0, 0)),
            pl.BlockSpec((None,) + w_out.shape[1:], lambda b, i: (layer, 0, 0), pipeline_mode=pl.Buffered(1)),
            pl.BlockSpec(w_router.shape, lambda b, i: (0, 0)),
            pl.BlockSpec((1, ne), lambda b, i: (0, 0)),
        ],
        out_specs=[tok, tok, small, small],
        out_shape=[
            jax.ShapeDtypeStruct((B, S, D), jnp.float32),
            jax.ShapeDtypeStruct((B, S, D), jnp.float32),
            jax.ShapeDtypeStruct((B, S, LANES), jnp.int32),
            jax.ShapeDtypeStruct((B, S, LANES), jnp.float32),
        ],
        scratch_shapes=[pltpu.VMEM(w_out.shape[1:], jnp.bfloat16), pltpu.VMEM((tm, D), jnp.bfloat16),
                        pltpu.VMEM(w_router.shape, jnp.bfloat16), pltpu.VMEM(w_router.shape, jnp.bfloat16)],
        compiler_params=_cparams(2),
        name="post_attn_router",
    )(oa, ob, x, mod6, mod6, mod6, beta, w_out, w_router, b_router.reshape(1, ne))


def _route(top_e, n_experts):
    T = top_e.shape[0]
    A = T * TOP_K
    onehot = (top_e[:, :, None] == jnp.arange(n_experts, dtype=jnp.int32)[None, None, :]).astype(jnp.int32)
    tot = jnp.sum(onehot, axis=1)
    incl = jnp.cumsum(tot, axis=0)
    counts = incl[-1]
    padded = ((counts + EXPERT_TILE - 1) // EXPERT_TILE) * EXPERT_TILE
    pend = jnp.cumsum(padded)
    pstart = pend - padded
    dest = jnp.sum(onehot * (incl - tot + pstart[None, :])[:, None, :], axis=2).astype(jnp.int32)
    n_slots = A + n_experts * EXPERT_TILE
    n_blocks = n_slots // EXPERT_TILE
    n_used = (pend[-1] // EXPERT_TILE).astype(jnp.int32)
    blk = jnp.arange(n_blocks, dtype=jnp.int32)
    be = jnp.sum((pend[None, :] <= (blk * EXPERT_TILE)[:, None]).astype(jnp.int32), axis=1)
    be = jnp.minimum(be, n_experts - 1)
    last = jnp.sum(jnp.where(blk == n_used - 1, be, 0))
    be = jnp.where(blk < n_used, be, last)
    group_end = jnp.sum(jnp.where(be[:, None] == jnp.arange(n_experts)[None, :], pend[None, :], 0), axis=1)
    after = group_end // EXPERT_TILE
    be_after = jnp.sum(jnp.where(blk[None, :] == after[:, None], be[None, :], 0), axis=1)
    nxt = jnp.where(after < n_used, be_after, -1).astype(jnp.int32)
    return dest, pend.astype(jnp.int32), be, nxt, n_used.reshape(1)


def _tile_major(dest, tm):
    T = dest.shape[0]
    return dest.reshape(T // tm, tm, TOP_K).transpose(0, 2, 1).reshape(T // tm, 1, TOP_K * tm)


def _issue_pairs(n, make_copy, wait=False):
    if not wait:
        for r in range(n):
            make_copy(r).start(priority=r % 2)
        return

    def body(u, c):
        make_copy(2 * u).wait()
        make_copy(2 * u + 1).wait()
        return c

    lax.fori_loop(0, n // 2, body, 0, unroll=4)


def _dispatch_kernel(pend_ref, dest_ref, h_ref, xs_hbm, zbuf, sem):
    i = pl.program_id(0)
    tm = dest_ref.shape[1] // TOP_K
    tile = zbuf.shape[0]
    ne = pend_ref.shape[0]

    @pl.when(i == 0)
    def _():
        zbuf[...] = jnp.zeros(zbuf.shape, zbuf.dtype)

        def fill(e):
            end = pend_ref[e]
            start = jnp.where(e == 0, 0, pend_ref[jnp.maximum(e - 1, 0)])
            last = pl.multiple_of(jnp.maximum(end - tile, 0), tile)
            return end > start, pltpu.make_async_copy(zbuf, xs_hbm.at[pl.ds(last, tile)], sem)

        def start(e, c):
            ok, cp = fill(e)

            @pl.when(ok)
            def _():
                cp.start()
            return c

        def wait(e, c):
            ok, cp = fill(e)

            @pl.when(ok)
            def _():
                cp.wait()
            return c

        lax.fori_loop(0, ne, start, 0)
        lax.fori_loop(0, ne, wait, 0)

        def tail(b):
            off = pl.multiple_of(b * tile, tile)
            return off >= pend_ref[ne - 1], pltpu.make_async_copy(zbuf, xs_hbm.at[pl.ds(off, tile)], sem)

        def tail_start(b, c):
            ok, cp = tail(b)

            @pl.when(ok)
            def _():
                cp.start()
            return c

        def tail_wait(b, c):
            ok, cp = tail(b)

            @pl.when(ok)
            def _():
                cp.wait()
            return c

        n_tiles = xs_hbm.shape[0] // tile
        lax.fori_loop(0, n_tiles, tail_start, 0)
        lax.fori_loop(0, n_tiles, tail_wait, 0)

    def row_copy(r):
        return pltpu.make_async_copy(h_ref.at[pl.ds(r % tm, 1)], xs_hbm.at[pl.ds(dest_ref[0, r], 1)], sem)

    _issue_pairs(TOP_K * tm, row_copy)
    _issue_pairs(TOP_K * tm, row_copy, wait=True)


def _dispatch(h2, dest, pend, n_slots, tm=MOE_ROWS):
    T, D = h2.shape
    grid_spec = pltpu.PrefetchScalarGridSpec(
        num_scalar_prefetch=1,
        grid=(T // tm,),
        in_specs=[
            pl.BlockSpec((None, 1, TOP_K * tm), lambda i, pe: (i, 0, 0), memory_space=pltpu.SMEM),
            pl.BlockSpec((tm, D), lambda i, pe: (i, 0)),
        ],
        out_specs=pl.BlockSpec(memory_space=pl.ANY),
        scratch_shapes=[pltpu.VMEM((EXPERT_TILE, D), h2.dtype), pltpu.SemaphoreType.DMA(())],
    )
    return pl.pallas_call(
        _dispatch_kernel,
        grid_spec=grid_spec,
        out_shape=jax.ShapeDtypeStruct((n_slots, D), h2.dtype),
        compiler_params=_cparams(1),
        name="moe_dispatch",
    )(pend, _tile_major(dest, tm), h2)


def _expert_kernel(be_ref, nxt_ref, nused_ref, x_ref, wu_hbm, bu_ref, wd_hbm, bd_ref, o_ref,
                   wu32, wd32, wub, wdb, sems, *, layer):
    i = pl.program_id(0)
    d_ff = wd32.shape[0]

    def fetch(e):
        return (pltpu.make_async_copy(wu_hbm.at[layer, e], wu32, sems.at[0]),
                pltpu.make_async_copy(wd_hbm.at[layer, e], wd32, sems.at[1]))

    @pl.when(i == 0)
    def _():
        for cp in fetch(be_ref[0]):
            cp.start()

    @pl.when(i < nused_ref[0])
    def _():
        @pl.when((i == 0) | (be_ref[i] != be_ref[jnp.maximum(i - 1, 0)]))
        def _():
            for cp in fetch(be_ref[i]):
                cp.wait()
            _cast_rows(wu32, wub)
            _cast_rows(wd32, wdb)

            @pl.when(nxt_ref[i] >= 0)
            def _():
                for cp in fetch(nxt_ref[i]):
                    cp.start()

        x = x_ref[...].astype(jnp.bfloat16)
        u = jnp.dot(x, wub[...], preferred_element_type=jnp.float32) + bu_ref[...]
        glu = jnp.minimum(u[:, :d_ff], SWIGLU_LIMIT)
        lin = jnp.clip(u[:, d_ff:], -SWIGLU_LIMIT, SWIGLU_LIMIT)
        act = glu * (1.0 / (1.0 + jnp.exp(-SWIGLU_ALPHA * glu))) * (lin + 1.0)
        o_ref[...] = jnp.dot(act.astype(jnp.bfloat16), wdb[...], preferred_element_type=jnp.float32) + bd_ref[...]

    @pl.when(i >= nused_ref[0])
    def _():
        o_ref[...] = jnp.zeros(o_ref.shape, o_ref.dtype)


def _experts(xs, be, nxt, n_used, layer, w_up, b_up, w_down, b_down):
    n_slots, D = xs.shape
    depth, ne, _, f2 = w_up.shape
    d_ff = w_down.shape[2]
    tm = EXPERT_TILE
    n_blocks = n_slots // tm
    grid_spec = pltpu.PrefetchScalarGridSpec(
        num_scalar_prefetch=3,
        grid=(n_blocks,),
        in_specs=[
            pl.BlockSpec((tm, D), lambda i, be, nx, nu: (jnp.minimum(i, nu[0] - 1), 0)),
            pl.BlockSpec(memory_space=pl.ANY),
            pl.BlockSpec((None, None, 1, f2), lambda i, be, nx, nu: (layer, be[i], 0, 0)),
            pl.BlockSpec(memory_space=pl.ANY),
            pl.BlockSpec((None, None, 1, D), lambda i, be, nx, nu: (layer, be[i], 0, 0)),
        ],
        out_specs=pl.BlockSpec((tm, D), lambda i, be, nx, nu: (i, 0)),
        scratch_shapes=[
            pltpu.VMEM((D, f2), jnp.float32),
            pltpu.VMEM((d_ff, D), jnp.float32),
            pltpu.VMEM((D, f2), jnp.bfloat16),
            pltpu.VMEM((d_ff, D), jnp.bfloat16),
            pltpu.SemaphoreType.DMA((2,)),
        ],
    )
    return pl.pallas_call(
        functools.partial(_expert_kernel, layer=layer),
        grid_spec=grid_spec,
        out_shape=jax.ShapeDtypeStruct((n_slots, D), jnp.float32),
        compiler_params=_cparams(1),
        name="experts",
    )(be, nxt, n_used, xs, w_up, b_up.reshape(depth, ne, 1, f2), w_down, b_down.reshape(depth, ne, 1, D))


def _combine_kernel(dest_ref, y_hbm, x_ref, gate_ref, tg_ref, gfin_ref, o_ref, ybuf, sem, *, final):
    tm = x_ref.shape[0]

    def row_copy(r):
        return pltpu.make_async_copy(y_hbm.at[pl.ds(dest_ref[0, r], 1)], ybuf.at[pl.ds(r, 1)], sem)

    _issue_pairs(TOP_K * tm, row_copy)
    _issue_pairs(TOP_K * tm, row_copy, wait=True)
    tg = tg_ref[...]
    y = None
    for k in range(TOP_K):
        part = tg[:, k:k + 1] * ybuf[k * tm:(k + 1) * tm, :]
        y = part if y is None else y + part
    x2 = x_ref[...] + gate_ref[...] * y
    if final:
        ms = jnp.mean(x2 * x2, axis=-1, keepdims=True)
        x2 = x2 * lax.rsqrt(ms + NORM_EPS) * gfin_ref[...]
    o_ref[...] = x2


def _combine(outs, dest, tg, x1, mod6, g_final, final, tm=MOE_ROWS):
    B, S, D = x1.shape
    nt = S // tm
    tok = pl.BlockSpec((None, tm, D), lambda b, i: (b, i, 0))
    return pl.pallas_call(
        functools.partial(_combine_kernel, final=final),
        grid=(B, nt),
        in_specs=[
            pl.BlockSpec((None, 1, TOP_K * tm), lambda b, i: (b * nt + i, 0, 0), memory_space=pltpu.SMEM),
            pl.BlockSpec(memory_space=pl.ANY),
            tok,
            pl.BlockSpec((None, None, 1, D), lambda b, i: (b, 5, 0, 0)),
            pl.BlockSpec((None, tm, LANES), lambda b, i: (b, i, 0)),
            pl.BlockSpec((1, D), lambda b, i: (0, 0)),
        ],
        out_specs=tok,
        out_shape=jax.ShapeDtypeStruct((B, S, D), jnp.float32),
        scratch_shapes=[pltpu.VMEM((TOP_K * tm, D), jnp.float32), pltpu.SemaphoreType.DMA(())],
        compiler_params=_cparams(2),
        name="moe_combine",
    )(_tile_major(dest, tm), outs, x1, mod6, tg, g_final.reshape(1, D))


def _rotary_tables(positions, scale_q):
    inv_freq = 1.0 / (ROPE_THETA ** (jnp.arange(0, HEAD_DIM, 2, dtype=jnp.float32) / HEAD_DIM))
    ang = positions[..., None].astype(jnp.float32) * inv_freq
    cos, sin = jnp.cos(ang), jnp.sin(ang)
    reps = LANES // HEAD_DIM
    cos_l = jnp.tile(jnp.concatenate([cos, cos], axis=-1), (1, 1, reps))
    sin_l = jnp.tile(jnp.concatenate([-sin, sin], axis=-1), (1, 1, reps))
    cos_t = jnp.stack([cos_l * scale_q, cos_l, jnp.ones_like(cos_l)])
    sin_t = jnp.stack([sin_l * scale_q, sin_l, jnp.zeros_like(sin_l)])
    return cos_t, sin_t


def kernel(x, c, positions, w_ada, b_ada, w_in, beta_a, beta_b, w_out, w_router, b_router, w_up, b_up, w_down, b_down, g_final):
    B, S, D = x.shape
    depth = w_ada.shape[0]
    ne = w_router.shape[-1]
    biases = tuple(jnp.asarray(b) for b in _dilated_biases())
    own_bias = jnp.asarray(_own_bias())

    xp = _permute_rows(x, inverse=False)
    pos_p = _to_perm(positions)
    cos_t, sin_t = _rotary_tables(pos_p, HEAD_DIM ** -0.5)
    mod = _ada_mod(c, w_ada, b_ada)

    for l in range(depth):
        mod6 = mod[l, :B].reshape(B, 6, 1, D)
        qkv = _qkv_proj(xp, mod6, w_in, l, cos_t, sin_t)
        oa = _dilated_attention(qkv, biases)
        ob = _moba_attention(qkv, own_bias)
        beta = jnp.concatenate([beta_a[l], beta_b[l]]).reshape(1, D)
        x1, h2, te, tg = _post_attention(oa, ob, xp, mod6, beta, w_out, l, w_router[l], b_router[l])
        dest, pend, be, nxt, n_used = _route(te.reshape(B * S, LANES)[:, :TOP_K], ne)
        xs = _dispatch(h2.reshape(B * S, D), dest, pend, be.shape[0] * EXPERT_TILE)
        outs = _experts(xs, be, nxt, n_used, l, w_up, b_up, w_down, b_down)
        xp = _combine(outs, dest, tg, x1, mod6, g_final, final=(l == depth - 1))

    return _permute_rows(xp, inverse=True)
```

```python
import functools
import math

import numpy as np
import jax
import jax.numpy as jnp
from jax import lax
from jax.experimental import pallas as pl
from jax.experimental.pallas import tpu as pltpu

HEAD_DIM = 64
LANES = 128
BLK = 256
BAND = 128
MOBA_TOPK = 3
TOP_K = 4
SWIGLU_LIMIT = 7.0
SWIGLU_ALPHA = 1.702
ROPE_THETA = 10000.0
NORM_EPS = 1e-6
NEG = -1e30
EXPERT_TILE = 256
QKV_ROWS = 512
POST_ROWS = 256
MOE_ROWS = 256
VMEM_LIMIT = 60000 * 1024

_ARB = pltpu.ARBITRARY


def _cparams(n_axes):
    return pltpu.CompilerParams(dimension_semantics=(_ARB,) * n_axes, vmem_limit_bytes=VMEM_LIMIT)


def _to_perm(x):
    B, S = x.shape[:2]
    rest = x.shape[2:]
    x = x.reshape(B, S // BLK, 16, 4, 4, *rest)
    x = jnp.swapaxes(x, 2, 4)
    return x.reshape(B, S, *rest)


def _from_perm(x):
    B, S = x.shape[:2]
    rest = x.shape[2:]
    x = x.reshape(B, S // BLK, 4, 4, 16, *rest)
    x = jnp.swapaxes(x, 2, 4)
    return x.reshape(B, S, *rest)


def _local_pos():
    p = np.arange(BLK)
    return 16 * (p % 16) + 4 * ((p // 16) % 4) + p // 64


def _band_bias(seq_q, seq_k, n_prev=None):
    dist = seq_q[:, None] - seq_k[None, :]
    ok = (dist >= 0) & (dist <= BAND)
    n_prev = seq_k.shape[0] // 2 if n_prev is None else n_prev
    exists = np.arange(seq_k.shape[0])[None, :] >= n_prev
    normal = np.where(ok, 0.0, NEG).astype(np.float32)
    first = np.where(ok & exists, 0.0, NEG).astype(np.float32)
    two = np.stack([first, normal])
    return np.concatenate([two, two], axis=1)


def _dilated_biases():
    lp = _local_pos()
    late = lp.reshape(16, 16)[:, 8:].reshape(-1)
    b1 = _band_bias(lp + BLK, np.concatenate([late, lp + BLK]), n_prev=late.shape[0])
    hi, c, lo = np.meshgrid(np.arange(4), np.arange(4), np.arange(16), indexing="ij")
    seq_k4 = (64 * hi + 4 * lo + c).reshape(-1)
    seq_q4 = seq_k4[:128] + 128
    b4 = _band_bias(seq_q4, seq_k4)
    b16 = _band_bias(np.arange(128) + 128, np.arange(256))
    return b1, b4, b16


def _own_bias():
    lp = _local_pos()
    c = np.where(lp[None, :] <= lp[:, None], 0.0, NEG).astype(np.float32)
    return np.concatenate([c, c, c, c], axis=0)


def _stack_heads(q):
    lane = lax.broadcasted_iota(jnp.int32, q.shape, 1)
    zero = jnp.zeros_like(q)
    return jnp.concatenate([jnp.where(lane < HEAD_DIM, q, zero), jnp.where(lane >= HEAD_DIM, q, zero)], axis=0)


def _scores(qs, kk):
    return lax.dot_general(qs, kk, (((1,), (1,)), ((), ())), preferred_element_type=jnp.float32)


def _tile_lanes(x, k):
    return x if k == 1 else jnp.concatenate([x] * k, axis=1)


def _softmax_step(s, vv, m_old, l_old, acc_old):
    m_cur = jnp.max(s, axis=1, keepdims=True)
    m_new = jnp.maximum(m_old, m_cur)
    alpha = jnp.exp(m_old - m_new)
    p = jnp.exp(s - _tile_lanes(m_new, s.shape[1] // LANES))
    l_new = alpha * l_old + jnp.sum(p, axis=1, keepdims=True)
    pv = jnp.dot(p.astype(vv.dtype), vv, preferred_element_type=jnp.float32)
    return m_new, l_new, alpha * acc_old + pv


def _merge_heads(x):
    n = x.shape[0] // 2
    lane = lax.broadcasted_iota(jnp.int32, (n, LANES), 1)
    return jnp.where(lane < HEAD_DIM, x[:n], x[n:])


def _ada_kernel(c_ref, w_ref, b_ref, o_ref):
    c = c_ref[...]
    sc = (c * (1.0 / (1.0 + jnp.exp(-c)))).astype(jnp.bfloat16)
    o_ref[...] = jnp.dot(sc, w_ref[...].astype(jnp.bfloat16), preferred_element_type=jnp.float32) + b_ref[...]


def _ada_mod(c, w_ada, b_ada):
    depth, d, n = w_ada.shape
    rows = 8
    tn = math.gcd(n, 1024)
    cp = jnp.zeros((rows, d), jnp.float32).at[: c.shape[0]].set(c)
    return pl.pallas_call(
        _ada_kernel,
        grid=(depth, n // tn),
        in_specs=[
            pl.BlockSpec((rows, d), lambda l, j: (0, 0)),
            pl.BlockSpec((None, d, tn), lambda l, j: (l, 0, j)),
            pl.BlockSpec((None, 1, tn), lambda l, j: (l, 0, j)),
        ],
        out_specs=pl.BlockSpec((None, rows, tn), lambda l, j: (l, 0, j)),
        out_shape=jax.ShapeDtypeStruct((depth, rows, n), jnp.float32),
        compiler_params=_cparams(2),
        name="ada_mod",
    )(cp, w_ada, b_ada.reshape(depth, 1, n))


def _rms_mod(x, scale, shift):
    ms = jnp.mean(x * x, axis=-1, keepdims=True)
    return x * lax.rsqrt(ms + NORM_EPS) * (1.0 + scale) + shift


def _cast_rows(src_ref, dst_ref):
    rows_per_step = math.gcd(src_ref.shape[0], 256)
    n = src_ref.shape[0] // rows_per_step

    def body(i, carry):
        r = pl.multiple_of(i * rows_per_step, rows_per_step)
        dst_ref[pl.ds(r, rows_per_step), :] = src_ref[pl.ds(r, rows_per_step), :].astype(dst_ref.dtype)
        return carry

    lax.fori_loop(0, n, body, 0)


def _qkv_kernel(x_ref, shift_ref, scale_ref, w_ref, cos_ref, sin_ref, o_ref, wbf_ref):
    b = pl.program_id(1)
    i = pl.program_id(2)

    @pl.when((b == 0) & (i == 0))
    def _():
        _cast_rows(w_ref, wbf_ref)

    h = _rms_mod(x_ref[...], scale_ref[...], shift_ref[...]).astype(jnp.bfloat16)
    acc = jnp.dot(h, wbf_ref[...], preferred_element_type=jnp.float32)
    cos = cos_ref[...]
    sin = sin_ref[...]
    lane = lax.broadcasted_iota(jnp.int32, cos.shape, 1)
    first_half = (lane % HEAD_DIM) < (HEAD_DIM // 2)
    for c in range(o_ref.shape[0]):
        a = acc[:, c * LANES:(c + 1) * LANES]
        partner = jnp.where(first_half, pltpu.roll(a, LANES - HEAD_DIM // 2, 1), pltpu.roll(a, HEAD_DIM // 2, 1))
        o_ref[c] = (a * cos + partner * sin).astype(o_ref.dtype)


def _qkv_proj(x, mod6, w_in, layer, cos_t, sin_t, tm=QKV_ROWS):
    B, S, D = x.shape
    n3 = w_in.shape[2]
    width = n3 // 3
    pairs = width // LANES
    return pl.pallas_call(
        _qkv_kernel,
        grid=(3, B, S // tm),
        in_specs=[
            pl.BlockSpec((None, tm, D), lambda j, b, i: (b, i, 0)),
            pl.BlockSpec((None, None, 1, D), lambda j, b, i: (b, 0, 0, 0)),
            pl.BlockSpec((None, None, 1, D), lambda j, b, i: (b, 1, 0, 0)),
            pl.BlockSpec((None, D, width), lambda j, b, i: (layer, 0, j), pipeline_mode=pl.Buffered(1)),
            pl.BlockSpec((None, None, tm, LANES), lambda j, b, i: (j, b, i, 0)),
            pl.BlockSpec((None, None, tm, LANES), lambda j, b, i: (j, b, i, 0)),
        ],
        out_specs=pl.BlockSpec((None, pairs, tm, LANES), lambda j, b, i: (b, j, i, 0)),
        out_shape=jax.ShapeDtypeStruct((B, 3 * pairs, S, LANES), jnp.bfloat16),
        scratch_shapes=[pltpu.VMEM((D, width), jnp.bfloat16)],
        compiler_params=_cparams(3),
        name="qkv_proj",
    )(x, mod6, mod6, w_in, cos_t, sin_t)


def _dilated_kernel(q_ref, k_ref, v_ref, b1_ref, b4_ref, b16_ref, o_ref, m_ref, l_ref, acc_ref):
    nb = q_ref.shape[0]

    def rows(x):
        return x.reshape(-1, LANES)

    def update(items):
        old = [(acc_ref[idx], m_ref[(0,) + idx], m_ref[(1,) + idx], l_ref[(0,) + idx], l_ref[(1,) + idx])
               for (_, _, _, _, idx) in items]
        new = []
        for (q, kk, vv, bias, idx), (a_raw, m0, m1, l0, l1) in zip(items, old):
            s = _scores(_stack_heads(q), kk) + bias
            m_old = jnp.concatenate([rows(m0), rows(m1)], axis=0)
            l_old = jnp.concatenate([rows(l0), rows(l1)], axis=0)
            a_old = jnp.concatenate([rows(a_raw), rows(a_raw)], axis=0)
            new.append(_softmax_step(s, vv, m_old, l_old, a_old) + (a_raw.shape, q.shape[0]))
        for (_, _, _, _, idx), (m_new, l_new, a_new, shp, n) in zip(items, new):
            m_ref[(0,) + idx] = m_new[:n].reshape(shp)
            m_ref[(1,) + idx] = m_new[n:].reshape(shp)
            l_ref[(0,) + idx] = l_new[:n].reshape(shp)
            l_ref[(1,) + idx] = l_new[n:].reshape(shp)
            acc_ref[idx] = _merge_heads(a_new).reshape(shp)

    def init(n, carry):
        m_ref[0, n] = jnp.full(m_ref.shape[2:], NEG, jnp.float32)
        m_ref[1, n] = jnp.full(m_ref.shape[2:], NEG, jnp.float32)
        l_ref[0, n] = jnp.zeros(l_ref.shape[2:], jnp.float32)
        l_ref[1, n] = jnp.zeros(l_ref.shape[2:], jnp.float32)
        acc_ref[n] = jnp.zeros(acc_ref.shape[1:], jnp.float32)
        return carry

    lax.fori_loop(0, nb, init, 0)

    def run(n_blocks, group, make_item):
        def step(t, carry):
            update([make_item(group * t + g) for g in range(group)])
            return carry

        lax.fori_loop(0, n_blocks // group, step, 0)

    def d1(n):
        prev = jnp.maximum(n - 1, 0)
        late = pl.ds(8, 8)
        kk = jnp.concatenate([rows(k_ref[prev, :, late, :]), rows(k_ref[n])], axis=0)
        vv = jnp.concatenate([rows(v_ref[prev, :, late, :]), rows(v_ref[n])], axis=0)
        return rows(q_ref[n]), kk, vv, b1_ref[jnp.minimum(n, 1)], (n,)

    run(nb, 4, d1)

    per4 = nb // 2

    def d4(t):
        r4 = t // per4
        a = t % per4
        cur = pl.ds(2 * a, 2)
        prev = pl.ds(jnp.maximum(2 * a - 2, 0), 2)
        grp = pl.ds(4 * r4, 4)
        kk = jnp.concatenate([rows(k_ref[prev, grp]), rows(k_ref[cur, grp])], axis=0)
        vv = jnp.concatenate([rows(v_ref[prev, grp]), rows(v_ref[cur, grp])], axis=0)
        return rows(q_ref[cur, grp]), kk, vv, b4_ref[jnp.minimum(a, 1)], (cur, grp)

    run(4 * per4, 16, d4)

    per16 = nb // 8

    def d16(t):
        rr = t // per16
        b = t % per16
        cur = pl.ds(8 * b, 8)
        prev = pl.ds(jnp.maximum(8 * b - 8, 0), 8)
        kk = jnp.concatenate([rows(k_ref[prev, rr]), rows(k_ref[cur, rr])], axis=0)
        vv = jnp.concatenate([rows(v_ref[prev, rr]), rows(v_ref[cur, rr])], axis=0)
        return rows(q_ref[cur, rr]), kk, vv, b16_ref[jnp.minimum(b, 1)], (cur, rr)

    run(16 * per16, 16, d16)

    def fin(n, carry):
        lane = lax.broadcasted_iota(jnp.int32, acc_ref.shape[1:], 2)
        den = jnp.where(lane < HEAD_DIM, l_ref[0, n], l_ref[1, n])
        o_ref[n] = (acc_ref[n] / den).astype(o_ref.dtype)
        return carry

    lax.fori_loop(0, nb, fin, 0)


def _seq_view(a):
    B, P, S, L = a.shape
    return a.reshape(B, P, S // BLK, 16, 16, L)


def _dilated_attention(qkv, biases):
    B, P3, S, _ = qkv.shape
    P = P3 // 3
    pa = P // 2
    nb = S // BLK
    assert S % (16 * BAND) == 0, "sequence must be a whole number of dilation-16 band blocks"
    v6 = _seq_view(qkv)
    blk = (None, None, nb, 16, 16, LANES)
    b1, b4, b16 = biases
    out = pl.pallas_call(
        _dilated_kernel,
        grid=(B, pa),
        in_specs=[
            pl.BlockSpec(blk, lambda b, p: (b, p, 0, 0, 0, 0)),
            pl.BlockSpec(blk, lambda b, p: (b, P + p, 0, 0, 0, 0)),
            pl.BlockSpec(blk, lambda b, p: (b, 2 * P + p, 0, 0, 0, 0)),
            pl.BlockSpec(b1.shape, lambda b, p: (0, 0, 0)),
            pl.BlockSpec(b4.shape, lambda b, p: (0, 0, 0)),
            pl.BlockSpec(b16.shape, lambda b, p: (0, 0, 0)),
        ],
        out_specs=pl.BlockSpec(blk, lambda b, p: (b, p, 0, 0, 0, 0)),
        out_shape=jax.ShapeDtypeStruct((B, pa, nb, 16, 16, LANES), jnp.bfloat16),
        scratch_shapes=[
            pltpu.VMEM((2, nb, 16, 16, LANES), jnp.float32),
            pltpu.VMEM((2, nb, 16, 16, LANES), jnp.float32),
            pltpu.VMEM((nb, 16, 16, LANES), jnp.float32),
        ],
        compiler_params=_cparams(2),
        name="dilated_attn",
    )(v6, v6, v6, b1, b4, b16)
    return out.reshape(B, pa, S, LANES)


def _moba_kernel(q_ref, k_ref, v_ref, own_ref, hot_ref, o_ref, kmean_ref, qa_ref, m_ref, l_ref, acc_ref):
    m = pl.program_id(2)
    nb = k_ref.shape[0]
    nbp = -(-nb // 8) * 8

    def rows(x):
        return x.reshape(-1, LANES)

    @pl.when(m == 0)
    def _():
        kmean_ref[...] = jnp.zeros(kmean_ref.shape, jnp.float32)
        for j in range(nb):
            kmean_ref[pl.ds(j, 1), :] = jnp.mean(rows(k_ref[j]).astype(jnp.float32), axis=0, keepdims=True)

    qs = jnp.concatenate([_stack_heads(rows(q_ref[0])), _stack_heads(rows(q_ref[1]))], axis=0)
    nq = qs.shape[0]
    half = nq // 2

    gate = _scores(kmean_ref[...].astype(jnp.bfloat16), qs)[:nbp]
    jidx = lax.broadcasted_iota(jnp.int32, gate.shape, 0)
    own = 2 * m + lax.broadcasted_iota(jnp.int32, gate.shape, 1) // half
    g = jnp.where(jidx < own, gate, -jnp.inf)
    rank = jnp.zeros(gate.shape, jnp.int32)
    for jp in range(nb):
        other = g[jp:jp + 1, :]
        beats = (other > g) | ((other == g) & (jp < jidx))
        rank = rank + beats.astype(jnp.int32)
    keep = (rank < MOBA_TOPK) & (jidx < own)
    pen_t = jnp.where(keep, 0.0, NEG).astype(jnp.float32)
    pen_t = jnp.concatenate([pen_t, jnp.full((LANES - nbp, nq), NEG, jnp.float32)], axis=0)
    qa_ref[...] = jnp.concatenate([qs, pen_t.T.astype(qs.dtype)], axis=1)

    ka, kb = rows(k_ref[2 * m]), rows(k_ref[2 * m + 1])
    s = jnp.concatenate([_scores(qs[:half], ka), _scores(qs[half:], kb)], axis=0) + own_ref[...]
    m0 = jnp.max(s, axis=1, keepdims=True)
    p = jnp.exp(s - m0)
    m_ref[...] = jnp.broadcast_to(m0, m_ref.shape)
    l_ref[...] = jnp.broadcast_to(jnp.sum(p, axis=1, keepdims=True), l_ref.shape)
    pb = p.astype(jnp.bfloat16)
    acc_ref[...] = jnp.concatenate(
        [jnp.dot(pb[:half], rows(v_ref[2 * m]), preferred_element_type=jnp.float32),
         jnp.dot(pb[half:], rows(v_ref[2 * m + 1]), preferred_element_type=jnp.float32)], axis=0)

    hi = pl.ds(half, half)
    sj = _scores(qa_ref[hi, :], jnp.concatenate([ka, hot_ref[2 * m]], axis=1))
    m_new, l_new, a_new = _softmax_step(sj, rows(v_ref[2 * m]), m_ref[hi, :], l_ref[hi, :], acc_ref[hi, :])
    m_ref[hi, :] = m_new
    l_ref[hi, :] = l_new
    acc_ref[hi, :] = a_new

    def past(t, carry):
        kk = jnp.concatenate([jnp.concatenate([rows(k_ref[2 * t]), hot_ref[2 * t]], axis=1),
                              jnp.concatenate([rows(k_ref[2 * t + 1]), hot_ref[2 * t + 1]], axis=1)], axis=0)
        vv = jnp.concatenate([rows(v_ref[2 * t]), rows(v_ref[2 * t + 1])], axis=0)
        m_new, l_new, a_new = _softmax_step(_scores(qa_ref[...], kk), vv, m_ref[...], l_ref[...], acc_ref[...])
        m_ref[...] = m_new
        l_ref[...] = l_new
        acc_ref[...] = a_new
        return carry

    lax.fori_loop(0, m, past, 0)

    out = acc_ref[...] / l_ref[...]
    o_ref[0] = _merge_heads(out[:half]).reshape(o_ref.shape[1:]).astype(o_ref.dtype)
    o_ref[1] = _merge_heads(out[half:]).reshape(o_ref.shape[1:]).astype(o_ref.dtype)


def _block_one_hot(nb):
    hot = np.zeros((nb, BLK, LANES), np.float32)
    hot[np.arange(nb), :, np.arange(nb)] = 1.0
    return jnp.asarray(hot, jnp.bfloat16)


def _moba_attention(qkv, own_bias):
    B, P3, S, _ = qkv.shape
    P = P3 // 3
    pa = P // 2
    nb = S // BLK
    assert nb <= LANES
    v6 = _seq_view(qkv)
    hot = _block_one_hot(nb)
    full = (None, None, nb, 16, 16, LANES)
    tile = (None, None, 2, 16, 16, LANES)
    nq = 4 * BLK
    out = pl.pallas_call(
        _moba_kernel,
        grid=(B, pa, nb // 2),
        in_specs=[
            pl.BlockSpec(tile, lambda b, p, n: (b, pa + p, n, 0, 0, 0)),
            pl.BlockSpec(full, lambda b, p, n: (b, P + pa + p, 0, 0, 0, 0)),
            pl.BlockSpec(full, lambda b, p, n: (b, 2 * P + pa + p, 0, 0, 0, 0)),
            pl.BlockSpec(own_bias.shape, lambda b, p, n: (0, 0)),
            pl.BlockSpec(hot.shape, lambda b, p, n: (0, 0, 0)),
        ],
        out_specs=pl.BlockSpec(tile, lambda b, p, n: (b, p, n, 0, 0, 0)),
        out_shape=jax.ShapeDtypeStruct((B, pa, nb, 16, 16, LANES), jnp.bfloat16),
        scratch_shapes=[
            pltpu.VMEM((LANES, LANES), jnp.float32),
            pltpu.VMEM((nq, 2 * LANES), jnp.bfloat16),
            pltpu.VMEM((nq, LANES), jnp.float32),
            pltpu.VMEM((nq, LANES), jnp.float32),
            pltpu.VMEM((nq, LANES), jnp.float32),
        ],
        compiler_params=_cparams(3),
        name="moba_attn",
    )(v6, v6, v6, own_bias, hot)
    return out.reshape(B, pa, S, LANES)


def _post_kernel(oa_ref, ob_ref, x_ref, gate_a_ref, shift_f_ref, scale_f_ref, beta_ref, w_ref, wr_ref, br_ref,
                 x1_ref, h2_ref, te_ref, tg_ref, wbf_ref, y_ref, whi_ref, wlo_ref):
    b = pl.program_id(0)
    i = pl.program_id(1)

    @pl.when((b == 0) & (i == 0))
    def _():
        _cast_rows(w_ref, wbf_ref)
        wr = wr_ref[...]
        hi = wr.astype(jnp.bfloat16)
        whi_ref[...] = hi
        wlo_ref[...] = (wr - hi.astype(jnp.float32)).astype(jnp.bfloat16)

    half = oa_ref.shape[0] * LANES
    for off, o_ref in ((0, oa_ref), (half, ob_ref)):
        ss = None
        for c in range(o_ref.shape[0]):
            o = o_ref[c].astype(jnp.float32)
            part = jnp.sum(o * o, axis=-1, keepdims=True)
            ss = part if ss is None else ss + part
        inv = lax.rsqrt(ss / half + NORM_EPS)
        for c in range(o_ref.shape[0]):
            lo = off + c * LANES
            y_ref[:, lo:lo + LANES] = (o_ref[c].astype(jnp.float32) * inv * beta_ref[:, lo:lo + LANES]).astype(y_ref.dtype)

    mixed = jnp.dot(y_ref[...], wbf_ref[...], preferred_element_type=jnp.float32)
    x1 = x_ref[...] + gate_a_ref[...] * mixed
    x1_ref[...] = x1
    h2 = _rms_mod(x1, scale_f_ref[...], shift_f_ref[...])
    h2_ref[...] = h2

    h_hi = h2.astype(jnp.bfloat16)
    h_lo = (h2 - h_hi.astype(jnp.float32)).astype(jnp.bfloat16)
    logits = (jnp.dot(h_hi, whi_ref[...], preferred_element_type=jnp.float32)
              + jnp.dot(h_hi, wlo_ref[...], preferred_element_type=jnp.float32)
              + jnp.dot(h_lo, whi_ref[...], preferred_element_type=jnp.float32)) + br_ref[...]
    lane = lax.broadcasted_iota(jnp.int32, logits.shape, 1).astype(jnp.float32)
    te = jnp.zeros(logits.shape, jnp.float32)
    tg = jnp.zeros(logits.shape, jnp.float32)
    work = logits
    top0 = None
    den = None
    for kk in range(TOP_K):
        mv = jnp.max(work, axis=1, keepdims=True)
        idx = jnp.min(jnp.where(work == mv, lane, float(LANES)), axis=1, keepdims=True)
        work = jnp.where(lane == idx, -jnp.inf, work)
        if kk == 0:
            top0 = mv
        e = jnp.exp(mv - top0)
        den = e if den is None else den + e
        te = jnp.where(lane == kk, idx, te)
        tg = jnp.where(lane == kk, e, tg)
    te_ref[...] = te.astype(jnp.int32)
    tg_ref[...] = tg / den


def _post_attention(oa, ob, x, mod6, beta, w_out, layer, w_router, b_router, tm=POST_ROWS):
    B, S, D = x.shape
    pa = oa.shape[1]
    ne = w_router.shape[1]
    assert TOP_K <= ne <= LANES
    w_router = jnp.pad(w_router, ((0, 0), (0, LANES - ne)))
    b_router = jnp.pad(b_router, (0, LANES - ne), constant_values=-jnp.inf)
    ne = LANES
    row = lambda k: pl.BlockSpec((None, None, 1, D), lambda b, i, k=k: (b, k, 0, 0))
    tok = pl.BlockSpec((None, tm, D), lambda b, i: (b, i, 0))
    small = pl.BlockSpec((None, tm, LANES), lambda b, i: (b, i, 0))
    return pl.pallas_call(
        _post_kernel,
        grid=(B, S // tm),
        in_specs=[
            pl.BlockSpec((None, pa, tm, LANES), lambda b, i: (b, 0, i, 0)),
            pl.BlockSpec((None, pa, tm, LANES), lambda b, i: (b, 0, i, 0)),
            tok, row(2), row(3), row(4),
            pl.BlockSpec((1, D), lambda b, i: (0, 0)),
            pl.BlockSpec((None,) + w_out.shape[1:], lambda b, i: (layer, 0, 0), pipeline_mode=pl.Buffered(1)),
            pl.BlockSpec(w_router.shape, lambda b, i: (0, 0)),
            pl.BlockSpec((1, ne), lambda b, i: (0, 0)),
        ],
        out_specs=[tok, tok, small, small],
        out_shape=[
            jax.ShapeDtypeStruct((B, S, D), jnp.float32),
            jax.ShapeDtypeStruct((B, S, D), jnp.float32),
            jax.ShapeDtypeStruct((B, S, LANES), jnp.int32),
            jax.ShapeDtypeStruct((B, S, LANES), jnp.float32),
        ],
        scratch_shapes=[pltpu.VMEM(w_out.shape[1:], jnp.bfloat16), pltpu.VMEM((tm, D), jnp.bfloat16),
                        pltpu.VMEM(w_router.shape, jnp.bfloat16), pltpu.VMEM(w_router.shape, jnp.bfloat16)],
        compiler_params=_cparams(2),
        name="post_attn_router",
    )(oa, ob, x, mod6, mod6, mod6, beta, w_out, w_router, b_router.reshape(1, ne))


def _route(top_e, n_experts):
    T = top_e.shape[0]
    A = T * TOP_K
    onehot = (top_e[:, :, None] == jnp.arange(n_experts, dtype=jnp.int32)[None, None, :]).astype(jnp.int32)
    tot = jnp.sum(onehot, axis=1)
    incl = jnp.cumsum(tot, axis=0)
    counts = incl[-1]
    padded = ((counts + EXPERT_TILE - 1) // EXPERT_TILE) * EXPERT_TILE
    pend = jnp.cumsum(padded)
    pstart = pend - padded
    dest = jnp.sum(onehot * (incl - tot + pstart[None, :])[:, None, :], axis=2).astype(jnp.int32)
    n_slots = A + n_experts * EXPERT_TILE
    n_blocks = n_slots // EXPERT_TILE
    n_used = (pend[-1] // EXPERT_TILE).astype(jnp.int32)
    blk = jnp.arange(n_blocks, dtype=jnp.int32)
    be = jnp.sum((pend[None, :] <= (blk * EXPERT_TILE)[:, None]).astype(jnp.int32), axis=1)
    be = jnp.minimum(be, n_experts - 1)
    last = jnp.sum(jnp.where(blk == n_used - 1, be, 0))
    be = jnp.where(blk < n_used, be, last)
    group_end = jnp.sum(jnp.where(be[:, None] == jnp.arange(n_experts)[None, :], pend[None, :], 0), axis=1)
    after = group_end // EXPERT_TILE
    be_after = jnp.sum(jnp.where(blk[None, :] == after[:, None], be[None, :], 0), axis=1)
    nxt = jnp.where(after < n_used, be_after, -1).astype(jnp.int32)
    return dest, pend.astype(jnp.int32), be, nxt, n_used.reshape(1)


def _tile_major(dest, tm):
    T = dest.shape[0]
    return dest.reshape(T // tm, tm, TOP_K).transpose(0, 2, 1).reshape(T // tm, 1, TOP_K * tm)


def _issue_pairs(n, make_copy, wait=False):
    if not wait:
        for r in range(n):
            make_copy(r).start(priority=r % 2)
        return

    def body(u, c):
        make_copy(2 * u).wait()
        make_copy(2 * u + 1).wait()
        return c

    lax.fori_loop(0, n // 2, body, 0, unroll=4)


def _dispatch_kernel(pend_ref, dest_ref, h_ref, xs_hbm, zbuf, sem):
    i = pl.program_id(0)
    tm = dest_ref.shape[1] // TOP_K
    tile = zbuf.shape[0]
    ne = pend_ref.shape[0]

    @pl.when(i == 0)
    def _():
        zbuf[...] = jnp.zeros(zbuf.shape, zbuf.dtype)

        def fill(e):
            end = pend_ref[e]
            start = jnp.where(e == 0, 0, pend_ref[jnp.maximum(e - 1, 0)])
            last = pl.multiple_of(jnp.maximum(end - tile, 0), tile)
            return end > start, pltpu.make_async_copy(zbuf, xs_hbm.at[pl.ds(last, tile)], sem)

        def start(e, c):
            ok, cp = fill(e)

            @pl.when(ok)
            def _():
                cp.start()
            return c

        def wait(e, c):
            ok, cp = fill(e)

            @pl.when(ok)
            def _():
                cp.wait()
            return c

        lax.fori_loop(0, ne, start, 0)
        lax.fori_loop(0, ne, wait, 0)

        def tail(b):
            off = pl.multiple_of(b * tile, tile)
            return off >= pend_ref[ne - 1], pltpu.make_async_copy(zbuf, xs_hbm.at[pl.ds(off, tile)], sem)

        def tail_start(b, c):
            ok, cp = tail(b)

            @pl.when(ok)
            def _():
                cp.start()
            return c

        def tail_wait(b, c):
            ok, cp = tail(b)

            @pl.when(ok)
            def _():
                cp.wait()
            return c

        n_tiles = xs_hbm.shape[0] // tile
        lax.fori_loop(0, n_tiles, tail_start, 0)
        lax.fori_loop(0, n_tiles, tail_wait, 0)

    def row_copy(r):
        return pltpu.make_async_copy(h_ref.at[pl.ds(r % tm, 1)], xs_hbm.at[pl.ds(dest_ref[0, r], 1)], sem)

    _issue_pairs(TOP_K * tm, row_copy)
    _issue_pairs(TOP_K * tm, row_copy, wait=True)


def _dispatch(h2, dest, pend, n_slots, tm=MOE_ROWS):
    T, D = h2.shape
    grid_spec = pltpu.PrefetchScalarGridSpec(
        num_scalar_prefetch=1,
        grid=(T // tm,),
        in_specs=[
            pl.BlockSpec((None, 1, TOP_K * tm), lambda i, pe: (i, 0, 0), memory_space=pltpu.SMEM),
            pl.BlockSpec((tm, D), lambda i, pe: (i, 0)),
        ],
        out_specs=pl.BlockSpec(memory_space=pl.ANY),
        scratch_shapes=[pltpu.VMEM((EXPERT_TILE, D), h2.dtype), pltpu.SemaphoreType.DMA(())],
    )
    return pl.pallas_call(
        _dispatch_kernel,
        grid_spec=grid_spec,
        out_shape=jax.ShapeDtypeStruct((n_slots, D), h2.dtype),
        compiler_params=_cparams(1),
        name="moe_dispatch",
    )(pend, _tile_major(dest, tm), h2)


def _expert_kernel(be_ref, nxt_ref, nused_ref, x_ref, wu_hbm, bu_ref, wd_hbm, bd_ref, o_ref,
                   wu32, wd32, wub, wdb, sems, *, layer):
    i = pl.program_id(0)
    d_ff = wd32.shape[0]

    def fetch(e):
        return (pltpu.make_async_copy(wu_hbm.at[layer, e], wu32, sems.at[0]),
                pltpu.make_async_copy(wd_hbm.at[layer, e], wd32, sems.at[1]))

    @pl.when(i == 0)
    def _():
        for cp in fetch(be_ref[0]):
            cp.start()

    @pl.when(i < nused_ref[0])
    def _():
        @pl.when((i == 0) | (be_ref[i] != be_ref[jnp.maximum(i - 1, 0)]))
        def _():
            for cp in fetch(be_ref[i]):
                cp.wait()
            _cast_rows(wu32, wub)
            _cast_rows(wd32, wdb)

            @pl.when(nxt_ref[i] >= 0)
            def _():
                for cp in fetch(nxt_ref[i]):
                    cp.start()

        x = x_ref[...].astype(jnp.bfloat16)
        u = jnp.dot(x, wub[...], preferred_element_type=jnp.float32) + bu_ref[...]
        glu = jnp.minimum(u[:, :d_ff], SWIGLU_LIMIT)
        lin = jnp.clip(u[:, d_ff:], -SWIGLU_LIMIT, SWIGLU_LIMIT)
        act = glu * (1.0 / (1.0 + jnp.exp(-SWIGLU_ALPHA * glu))) * (lin + 1.0)
        o_ref[...] = jnp.dot(act.astype(jnp.bfloat16), wdb[...], preferred_element_type=jnp.float32) + bd_ref[...]

    @pl.when(i >= nused_ref[0])
    def _():
        o_ref[...] = jnp.zeros(o_ref.shape, o_ref.dtype)


def _experts(xs, be, nxt, n_used, layer, w_up, b_up, w_down, b_down):
    n_slots, D = xs.shape
    depth, ne, _, f2 = w_up.shape
    d_ff = w_down.shape[2]
    tm = EXPERT_TILE
    n_blocks = n_slots // tm
    grid_spec = pltpu.PrefetchScalarGridSpec(
        num_scalar_prefetch=3,
        grid=(n_blocks,),
        in_specs=[
            pl.BlockSpec((tm, D), lambda i, be, nx, nu: (jnp.minimum(i, nu[0] - 1), 0)),
            pl.BlockSpec(memory_space=pl.ANY),
            pl.BlockSpec((None, None, 1, f2), lambda i, be, nx, nu: (layer, be[i], 0, 0)),
            pl.BlockSpec(memory_space=pl.ANY),
            pl.BlockSpec((None, None, 1, D), lambda i, be, nx, nu: (layer, be[i], 0, 0)),
        ],
        out_specs=pl.BlockSpec((tm, D), lambda i, be, nx, nu: (i, 0)),
        scratch_shapes=[
            pltpu.VMEM((D, f2), jnp.float32),
            pltpu.VMEM((d_ff, D), jnp.float32),
            pltpu.VMEM((D, f2), jnp.bfloat16),
            pltpu.VMEM((d_ff, D), jnp.bfloat16),
            pltpu.SemaphoreType.DMA((2,)),
        ],
    )
    return pl.pallas_call(
        functools.partial(_expert_kernel, layer=layer),
        grid_spec=grid_spec,
        out_shape=jax.ShapeDtypeStruct((n_slots, D), jnp.float32),
        compiler_params=_cparams(1),
        name="experts",
    )(be, nxt, n_used, xs, w_up, b_up.reshape(depth, ne, 1, f2), w_down, b_down.reshape(depth, ne, 1, D))


def _combine_kernel(dest_ref, y_hbm, x_ref, gate_ref, tg_ref, gfin_ref, o_ref, ybuf, sem, *, final):
    tm = x_ref.shape[0]

    def row_copy(r):
        return pltpu.make_async_copy(y_hbm.at[pl.ds(dest_ref[0, r], 1)], ybuf.at[pl.ds(r, 1)], sem)

    _issue_pairs(TOP_K * tm, row_copy)
    _issue_pairs(TOP_K * tm, row_copy, wait=True)
    tg = tg_ref[...]
    y = None
    for k in range(TOP_K):
        part = tg[:, k:k + 1] * ybuf[k * tm:(k + 1) * tm, :]
        y = part if y is None else y + part
    x2 = x_ref[...] + gate_ref[...] * y
    if final:
        ms = jnp.mean(x2 * x2, axis=-1, keepdims=True)
        x2 = x2 * lax.rsqrt(ms + NORM_EPS) * gfin_ref[...]
    o_ref[...] = x2


def _combine(outs, dest, tg, x1, mod6, g_final, final, tm=MOE_ROWS):
    B, S, D = x1.shape
    nt = S // tm
    tok = pl.BlockSpec((None, tm, D), lambda b, i: (b, i, 0))
    return pl.pallas_call(
        functools.partial(_combine_kernel, final=final),
        grid=(B, nt),
        in_specs=[
            pl.BlockSpec((None, 1, TOP_K * tm), lambda b, i: (b * nt + i, 0, 0), memory_space=pltpu.SMEM),
            pl.BlockSpec(memory_space=pl.ANY),
            tok,
            pl.BlockSpec((None, None, 1, D), lambda b, i: (b, 5, 0, 0)),
            pl.BlockSpec((None, tm, LANES), lambda b, i: (b, i, 0)),
            pl.BlockSpec((1, D), lambda b, i: (0, 0)),
        ],
        out_specs=tok,
        out_shape=jax.ShapeDtypeStruct((B, S, D), jnp.float32),
        scratch_shapes=[pltpu.VMEM((TOP_K * tm, D), jnp.float32), pltpu.SemaphoreType.DMA(())],
        compiler_params=_cparams(2),
        name="moe_combine",
    )(_tile_major(dest, tm), outs, x1, mod6, tg, g_final.reshape(1, D))


def _rotary_tables(positions, scale_q):
    inv_freq = 1.0 / (ROPE_THETA ** (jnp.arange(0, HEAD_DIM, 2, dtype=jnp.float32) / HEAD_DIM))
    ang = positions[..., None].astype(jnp.float32) * inv_freq
    cos, sin = jnp.cos(ang), jnp.sin(ang)
    reps = LANES // HEAD_DIM
    cos_l = jnp.tile(jnp.concatenate([cos, cos], axis=-1), (1, 1, reps))
    sin_l = jnp.tile(jnp.concatenate([-sin, sin], axis=-1), (1, 1, reps))
    cos_t = jnp.stack([cos_l * scale_q, cos_l, jnp.ones_like(cos_l)])
    sin_t = jnp.stack([sin_l * scale_q, sin_l, jnp.zeros_like(sin_l)])
    return cos_t, sin_t


def kernel(x, c, positions, w_ada, b_ada, w_in, beta_a, beta_b, w_out, w_router, b_router, w_up, b_up, w_down, b_down, g_final):
    B, S, D = x.shape
    depth = w_ada.shape[0]
    ne = w_router.shape[-1]
    biases = tuple(jnp.asarray(b) for b in _dilated_biases())
    own_bias = jnp.asarray(_own_bias())

    xp = _to_perm(x)
    pos_p = _to_perm(positions)
    cos_t, sin_t = _rotary_tables(pos_p, HEAD_DIM ** -0.5)
    mod = _ada_mod(c, w_ada, b_ada)

    for l in range(depth):
        mod6 = mod[l, :B].reshape(B, 6, 1, D)
        qkv = _qkv_proj(xp, mod6, w_in, l, cos_t, sin_t)
        oa = _dilated_attention(qkv, biases)
        ob = _moba_attention(qkv, own_bias)
        beta = jnp.concatenate([beta_a[l], beta_b[l]]).reshape(1, D)
        x1, h2, te, tg = _post_attention(oa, ob, xp, mod6, beta, w_out, l, w_router[l], b_router[l])
        dest, pend, be, nxt, n_used = _route(te.reshape(B * S, LANES)[:, :TOP_K], ne)
        xs = _dispatch(h2.reshape(B * S, D), dest, pend, be.shape[0] * EXPERT_TILE)
        outs = _experts(xs, be, nxt, n_used, l, w_up, b_up, w_down, b_down)
        xp = _combine(outs, dest, tg, x1, mod6, g_final, final=(l == depth - 1))

    return _from_perm(xp)
```
